```python
import jax, jax.numpy as jnp
from jax import lax
import numpy as np

D_MODEL = 2048
BATCH = 1
SEQ = 8192
DEPTH = 2
DEC_BATCH = 32
DEC_SEQ = 1
PAST_LEN = 8192
PAGE_SIZE = 128

HEAD_DIM = 128
N_HEADS = D_MODEL // HEAD_DIM
ATTN_BLOCK = 128
ATTN_SCALE = HEAD_DIM ** -0.5
FGATE_BIAS_MEAN = 2.0
NEG_INF = -1e30
N_ATTN = (DEPTH + 1) // 2
N_POOL = DEPTH // 2
POOL_WINDOWS = (2, 4, 8, 16)
POOL_GROUPS = len(POOL_WINDOWS)
POOL_GROUP_DIM = D_MODEL // POOL_GROUPS
POOL_STATE = max(POOL_WINDOWS) - 1
PEER_HEADS = 8
PEER_NKEYS = 128
PEER_EXPERTS = PEER_NKEYS * PEER_NKEYS
PEER_TOPK = 16
PEER_QDIM = 256
PEER_QHALF = PEER_QDIM // 2
PEER_BLOCK = 128
DEEPNORM_ALPHA = (2.0 * DEPTH) ** 0.25
DEEPNORM_BETA = (8.0 * DEPTH) ** -0.25
LN_EPS = 1e-5

kernel_name = 'fox_pool_peer_hybrid_step'


def layer_norm(x, g, b):
    xf = x.astype(jnp.float32)
    mu = jnp.mean(xf, axis=-1, keepdims=True)
    var = jnp.mean(jnp.square(xf - mu), axis=-1, keepdims=True)
    return ((xf - mu) * lax.rsqrt(var + LN_EPS) * g + b).astype(x.dtype)


def _fox_in(x, w_in, b_f):
    proj = x @ w_in
    hs = x.shape[:-1] + (N_HEADS, HEAD_DIM)
    q = proj[..., :D_MODEL].reshape(hs)
    k = proj[..., D_MODEL:2 * D_MODEL].reshape(hs)
    v = proj[..., 2 * D_MODEL:3 * D_MODEL].reshape(hs)
    logf = jax.nn.log_sigmoid((proj[..., 3 * D_MODEL:] + b_f).astype(jnp.float32))
    return q, k, v, logf


def fox_prompt(x, w_in, b_f, w_o):
    b, s, _ = x.shape
    q, k, v, logf = _fox_in(x, w_in, b_f)
    csum = jnp.cumsum(logf, axis=1)
    csum_k = csum.transpose(0, 2, 1)[:, :, None, :]
    nb = s // ATTN_BLOCK
    q_blk = q.reshape(b, nb, ATTN_BLOCK, N_HEADS, HEAD_DIM).transpose(1, 0, 2, 3, 4)
    c_blk = csum.reshape(b, nb, ATTN_BLOCK, N_HEADS).transpose(1, 0, 3, 2)
    kpos = jnp.arange(s)

    def block(args):
        qi, ci, bi = args
        qpos = bi * ATTN_BLOCK + jnp.arange(ATTN_BLOCK)
        sc = jnp.einsum('bqhd,bkhd->bhqk', qi, k, preferred_element_type=jnp.float32) * ATTN_SCALE
        sc = sc + ci[..., None] - csum_k
        sc = jnp.where(kpos[None, None, None, :] <= qpos[None, None, :, None], sc, NEG_INF)
        p = jax.nn.softmax(sc, axis=-1).astype(v.dtype)
        return jnp.einsum('bhqk,bkhd->bqhd', p, v)

    o = lax.map(block, (q_blk, c_blk, jnp.arange(nb)))
    o = o.transpose(1, 0, 2, 3, 4).reshape(b, s, D_MODEL)
    return o @ w_o, k, v, logf


def fox_sample(x, cache_k, cache_v, cache_logf, page_table, a, w_in, b_f, w_o):
    db, t, _ = x.shape
    past = page_table.shape[1] * PAGE_SIZE
    q, k, v, logf = _fox_in(x, w_in, b_f)
    qpos = past + jnp.arange(t)
    kpos = jnp.arange(past + t)
    mask = kpos[None, None, :] <= qpos[None, :, None]

    def one(args):
        qi, ki, vi, li, pt = args
        kp = cache_k[a, pt].reshape(past, N_HEADS, HEAD_DIM).astype(ki.dtype)
        vp = cache_v[a, pt].reshape(past, N_HEADS, HEAD_DIM).astype(vi.dtype)
        lp = cache_logf[a, pt].reshape(past, N_HEADS).astype(jnp.float32)
        kk = jnp.concatenate([kp, ki], axis=0)
        vv = jnp.concatenate([vp, vi], axis=0)
        csum = jnp.cumsum(jnp.concatenate([lp, li], axis=0), axis=0)
        sc = jnp.einsum('qhd,khd->hqk', qi, kk, preferred_element_type=jnp.float32) * ATTN_SCALE
        sc = sc + csum[past:].T[:, :, None] - csum.T[:, None, :]
        sc = jnp.where(mask, sc, NEG_INF)
        p = jax.nn.softmax(sc, axis=-1).astype(vv.dtype)
        return jnp.einsum('hqk,khd->qhd', p, vv)

    o = lax.map(one, (q, k, v, logf, page_table))
    return o.reshape(db, t, D_MODEL) @ w_o, k, v, logf


def pool_mix(x, prefix, pos0, w_grp, scale):
    b, t, _ = x.shape
    xx = jnp.concatenate([prefix.astype(x.dtype), x], axis=1)
    xf = xx.astype(jnp.float32)
    cs = jnp.concatenate([jnp.zeros((b, 1, D_MODEL), jnp.float32), jnp.cumsum(xf, axis=1)], axis=1)
    pos = pos0 + jnp.arange(t)
    p0 = POOL_STATE
    outs = []
    for g, w in enumerate(POOL_WINDOWS):
        sl = slice(g * POOL_GROUP_DIM, (g + 1) * POOL_GROUP_DIM)
        win = cs[:, p0 + 1:p0 + 1 + t, sl] - cs[:, p0 + 1 - w:p0 + 1 - w + t, sl]
        cnt = jnp.minimum(pos + 1, w).astype(jnp.float32)[None, :, None]
        z = win / cnt - xf[:, p0:, sl]
        outs.append(jnp.einsum('btc,cd->btd', z.astype(x.dtype), w_grp[g]))
    y = jnp.concatenate(outs, axis=-1) * scale
    return y, xx[:, -POOL_STATE:]


def peer_ffn(x, l, w_q, keys1, keys2, u_tab, v_tab):
    shp = x.shape
    xf = x.reshape(-1, D_MODEL)
    n = xf.shape[0]
    q = (xf @ w_q[l]).reshape(n, PEER_HEADS, PEER_QDIM)
    s1 = jnp.einsum('nhc,hkc->nhk', q[..., :PEER_QHALF], keys1[l], preferred_element_type=jnp.float32)
    s2 = jnp.einsum('nhc,hkc->nhk', q[..., PEER_QHALF:], keys2[l], preferred_element_type=jnp.float32)
    t1, i1 = lax.top_k(s1, PEER_TOPK)
    t2, i2 = lax.top_k(s2, PEER_TOPK)
    cand = (t1[..., :, None] + t2[..., None, :]).reshape(n, PEER_HEADS, PEER_TOPK * PEER_TOPK)
    cidx = (i1[..., :, None] * PEER_NKEYS + i2[..., None, :]).reshape(n, PEER_HEADS, PEER_TOPK * PEER_TOPK)
    sc, sel = lax.top_k(cand, PEER_TOPK)
    eid = jnp.take_along_axis(cidx, sel, axis=-1)
    gate = jax.nn.softmax(sc, axis=-1)
    nb = -(-n // PEER_BLOCK)
    pad = nb * PEER_BLOCK - n
    xp = jnp.pad(xf, ((0, pad), (0, 0))).reshape(nb, PEER_BLOCK, D_MODEL)
    ep = jnp.pad(eid, ((0, pad), (0, 0), (0, 0))).reshape(nb, PEER_BLOCK, PEER_HEADS, PEER_TOPK)
    gp = jnp.pad(gate, ((0, pad), (0, 0), (0, 0))).reshape(nb, PEER_BLOCK, PEER_HEADS, PEER_TOPK)

    def block(args):
        xb, eb, gb = args
        ug = u_tab[l, eb]
        h = jnp.einsum('nd,nhkd->nhk', xb, ug, preferred_element_type=jnp.float32)
        act = (jax.nn.gelu(h, approximate=False) * gb).astype(xb.dtype)
        vg = v_tab[l, eb]
        return jnp.einsum('nhk,nhkd->nd', act, vg)

    out = lax.map(block, (xp, ep, gp)).reshape(nb * PEER_BLOCK, D_MODEL)[:n]
    return out.reshape(shp)


def setup_inputs(seed: int = 0) -> dict:
    key = jax.random.key(seed)
    ks = jax.random.split(key, 24)
    nrm = jax.random.normal
    n_pages = PAST_LEN // PAGE_SIZE
    n_used = DEC_BATCH * n_pages
    n_phys = n_used + (n_used + 3) // 4
    d = D_MODEL
    x_prompt = nrm(ks[0], (BATCH, SEQ, d), jnp.float32)
    x_sample = nrm(ks[1], (DEC_BATCH, DEC_SEQ, d), jnp.float32)
    cache_k = nrm(ks[2], (N_ATTN, n_phys, PAGE_SIZE, N_HEADS, HEAD_DIM), jnp.float32)
    cache_v = DEEPNORM_BETA * nrm(ks[3], (N_ATTN, n_phys, PAGE_SIZE, N_HEADS, HEAD_DIM), jnp.float32)
    cache_logf = jax.nn.log_sigmoid(FGATE_BIAS_MEAN + nrm(ks[4], (N_ATTN, n_phys, PAGE_SIZE, N_HEADS), jnp.float32))
    page_table = jax.random.permutation(ks[5], n_phys)[:n_used].reshape(DEC_BATCH, n_pages).astype(jnp.int32)
    state_pool = nrm(ks[6], (N_POOL, DEC_BATCH, POOL_STATE, d), jnp.float32)
    col_scale = jnp.concatenate([jnp.ones((2 * d,), jnp.float32),
                                 jnp.full((d,), DEEPNORM_BETA, jnp.float32),
                                 jnp.ones((N_HEADS,), jnp.float32)])
    attn_w_in = nrm(ks[7], (N_ATTN, d, 3 * d + N_HEADS), jnp.float32) * d ** -0.5 * col_scale
    attn_b_f = FGATE_BIAS_MEAN + 0.1 * nrm(ks[8], (N_ATTN, N_HEADS), jnp.float32)
    attn_w_o = nrm(ks[9], (N_ATTN, d, d), jnp.float32) * d ** -0.5 * DEEPNORM_BETA
    pool_w = nrm(ks[10], (N_POOL, POOL_GROUPS, POOL_GROUP_DIM, POOL_GROUP_DIM), jnp.float32) * POOL_GROUP_DIM ** -0.5 * DEEPNORM_BETA
    pool_scale = 1.0 + 0.02 * nrm(ks[11], (N_POOL, d), jnp.float32)
    peer_w_q = nrm(ks[12], (DEPTH, d, PEER_HEADS * PEER_QDIM), jnp.float32) * d ** -0.5
    peer_keys1 = nrm(ks[13], (DEPTH, PEER_HEADS, PEER_NKEYS, PEER_QHALF), jnp.float32) * PEER_QHALF ** -0.5
    peer_keys2 = nrm(ks[14], (DEPTH, PEER_HEADS, PEER_NKEYS, PEER_QHALF), jnp.float32) * PEER_QHALF ** -0.5
    peer_u = nrm(ks[15], (DEPTH, PEER_EXPERTS, d), jnp.float32) * d ** -0.5
    peer_v = nrm(ks[16], (DEPTH, PEER_EXPERTS, d), jnp.float32) * PEER_HEADS ** -0.5 * DEEPNORM_BETA
    ln_mix_g = 1.0 + 0.02 * nrm(ks[17], (DEPTH, d), jnp.float32)
    ln_mix_b = 0.02 * nrm(ks[18], (DEPTH, d), jnp.float32)
    ln_ffn_g = 1.0 + 0.02 * nrm(ks[19], (DEPTH, d), jnp.float32)
    ln_ffn_b = 0.02 * nrm(ks[20], (DEPTH, d), jnp.float32)
    return {'x_prompt': x_prompt, 'x_sample': x_sample, 'cache_k': cache_k, 'cache_v': cache_v,
            'cache_logf': cache_logf, 'page_table': page_table, 'state_pool': state_pool,
            'attn_w_in': attn_w_in, 'attn_b_f': attn_b_f, 'attn_w_o': attn_w_o,
            'pool_w': pool_w, 'pool_scale': pool_scale, 'peer_w_q': peer_w_q,
            'peer_keys1': peer_keys1, 'peer_keys2': peer_keys2, 'peer_u': peer_u, 'peer_v': peer_v,
            'ln_mix_g': ln_mix_g, 'ln_mix_b': ln_mix_b, 'ln_ffn_g': ln_ffn_g, 'ln_ffn_b': ln_ffn_b}


def reference(x_prompt, x_sample, cache_k, cache_v, cache_logf, page_table, state_pool,
              attn_w_in, attn_b_f, attn_w_o, pool_w, pool_scale, peer_w_q, peer_keys1,
              peer_keys2, peer_u, peer_v, ln_mix_g, ln_mix_b, ln_ffn_g, ln_ffn_b):
    past_len = page_table.shape[1] * PAGE_SIZE
    xp, xs = x_prompt, x_sample
    kp_l, vp_l, lp_l, ks_l, vs_l, ls_l, pp_l, ps_l = [], [], [], [], [], [], [], []
    for i in range(DEPTH):
        j = i // 2
        if i % 2 == 0:
            mp, kp, vp, lp = fox_prompt(xp, attn_w_in[j], attn_b_f[j], attn_w_o[j])
            ms, ks_, vs, ls = fox_sample(xs, cache_k, cache_v, cache_logf, page_table, j,
                                         attn_w_in[j], attn_b_f[j], attn_w_o[j])
            kp_l.append(kp); vp_l.append(vp); lp_l.append(lp.astype(cache_logf.dtype))
            ks_l.append(ks_); vs_l.append(vs); ls_l.append(ls.astype(cache_logf.dtype))
        else:
            zero_prefix = jnp.zeros((xp.shape[0], POOL_STATE, D_MODEL), xp.dtype)
            mp, sp = pool_mix(xp, zero_prefix, 0, pool_w[j], pool_scale[j])
            ms, ss = pool_mix(xs, state_pool[j], past_len, pool_w[j], pool_scale[j])
            pp_l.append(sp); ps_l.append(ss)
        xp = layer_norm(DEEPNORM_ALPHA * xp + mp, ln_mix_g[i], ln_mix_b[i])
        xs = layer_norm(DEEPNORM_ALPHA * xs + ms, ln_mix_g[i], ln_mix_b[i])
        fp = peer_ffn(xp, i, peer_w_q, peer_keys1, peer_keys2, peer_u, peer_v)
        fs = peer_ffn(xs, i, peer_w_q, peer_keys1, peer_keys2, peer_u, peer_v)
        xp = layer_norm(DEEPNORM_ALPHA * xp + fp, ln_ffn_g[i], ln_ffn_b[i])
        xs = layer_norm(DEEPNORM_ALPHA * xs + fs, ln_ffn_g[i], ln_ffn_b[i])
    return (xp, xs, jnp.stack(kp_l), jnp.stack(vp_l), jnp.stack(lp_l),
            jnp.stack(ks_l), jnp.stack(vs_l), jnp.stack(ls_l),
            jnp.stack(pp_l), jnp.stack(ps_l))
```

```python
import functools

import jax
import jax.numpy as jnp
from jax import lax
from jax.experimental import pallas as pl
from jax.experimental.pallas import tpu as pltpu

BF16 = jnp.bfloat16
F32 = jnp.float32

HEAD_DIM = 128
PAGE_SIZE = 128
ATTN_SCALE = HEAD_DIM ** -0.5
NEG_INF = -1e30
POOL_WINDOWS = (2, 4, 8, 16)
POOL_STATE = max(POOL_WINDOWS) - 1
PEER_HEADS = 8
PEER_NKEYS = 128
PEER_TOPK = 16
PEER_QDIM = 256
PEER_QHALF = PEER_QDIM // 2
LN_EPS = 1e-5

LANES = 128
SUBLANES = 8
V7X_VMEM_BYTES = 64 * 1024 * 1024
VMEM_LIMIT = V7X_VMEM_BYTES * 7 // 8

NT_DIMS = (((1,), (1,)), ((), ()))


def _params(semantics):
    return pltpu.CompilerParams(dimension_semantics=semantics, vmem_limit_bytes=VMEM_LIMIT)


def _layer_norm(y, g, b):
    mu = jnp.mean(y, axis=-1, keepdims=True)
    d = y - mu
    var = jnp.mean(d * d, axis=-1, keepdims=True)
    return d * lax.rsqrt(var + LN_EPS) * g + b


def _log_sigmoid(z):
    return jnp.minimum(z, 0.0) - jnp.log1p(jnp.exp(-jnp.abs(z)))


def _cumsum_lanes(c):
    n = c.shape[-1]
    lane = lax.broadcasted_iota(jnp.int32, c.shape, c.ndim - 1)
    s = 1
    while s < n:
        c = c + jnp.where(lane >= s, pltpu.roll(c, s, axis=c.ndim - 1), 0.0)
        s *= 2
    return c


def _fox_proj_kernel(x_ref, wq_ref, wk_ref, wv_ref, wf_ref, wft_ref, bf_ref, bft_ref,
                     qb_ref, k_ref, kb_ref, v_ref, vb_ref, lf_ref, lft_ref):
    xb = x_ref[...].astype(BF16)
    qb_ref[...] = jnp.dot(xb, wq_ref[...], preferred_element_type=F32).astype(BF16)
    k = jnp.dot(xb, wk_ref[...], preferred_element_type=F32)
    k_ref[...] = k
    kb_ref[...] = k.astype(BF16)
    v = jnp.dot(xb, wv_ref[...], preferred_element_type=F32)
    v_ref[...] = v
    vb_ref[...] = v.astype(BF16)

    @pl.when(pl.program_id(1) == 0)
    def _():
        z = jnp.dot(xb, wf_ref[...], preferred_element_type=F32) + bf_ref[...]
        lf_ref[...] = _log_sigmoid(z)
        zt = lax.dot_general(wft_ref[...], xb, NT_DIMS, preferred_element_type=F32) + bft_ref[...]
        lft_ref[...] = _log_sigmoid(zt)


def _fox_proj(x, wq, wk, wv, wf, wft, b_f, b_ft, *, tm, tn):
    n, d = x.shape
    h = wf.shape[1]
    row_col = lambda i, j: (i, j)
    return pl.pallas_call(
        _fox_proj_kernel,
        grid=(n // tm, d // tn),
        in_specs=[
            pl.BlockSpec((tm, d), lambda i, j: (i, 0)),
            pl.BlockSpec((d, tn), lambda i, j: (0, j)),
            pl.BlockSpec((d, tn), lambda i, j: (0, j)),
            pl.BlockSpec((d, tn), lambda i, j: (0, j)),
            pl.BlockSpec((d, h), lambda i, j: (0, 0)),
            pl.BlockSpec((h, d), lambda i, j: (0, 0)),
            pl.BlockSpec((1, h), lambda i, j: (0, 0)),
            pl.BlockSpec((h, 1), lambda i, j: (0, 0)),
        ],
        out_specs=[
            pl.BlockSpec((tm, tn), row_col),
            pl.BlockSpec((tm, tn), row_col),
            pl.BlockSpec((tm, tn), row_col),
            pl.BlockSpec((tm, tn), row_col),
            pl.BlockSpec((tm, tn), row_col),
            pl.BlockSpec((tm, h), lambda i, j: (i, 0)),
            pl.BlockSpec((h, tm), lambda i, j: (0, i)),
        ],
        out_shape=[
            jax.ShapeDtypeStruct((n, d), BF16),
            jax.ShapeDtypeStruct((n, d), F32),
            jax.ShapeDtypeStruct((n, d), BF16),
            jax.ShapeDtypeStruct((n, d), F32),
            jax.ShapeDtypeStruct((n, d), BF16),
            jax.ShapeDtypeStruct((n, h), F32),
            jax.ShapeDtypeStruct((h, n), F32),
        ],
        compiler_params=_params(("parallel", "arbitrary")),
        name="fox_proj",
    )(x, wq, wk, wv, wf, wft, b_f, b_ft)


def _cumsum_kernel(x_ref, o_ref):
    o_ref[...] = _cumsum_lanes(x_ref[...])


def _cumsum_rows_of(xt):
    return pl.pallas_call(
        _cumsum_kernel,
        out_shape=jax.ShapeDtypeStruct(xt.shape, F32),
        compiler_params=pltpu.CompilerParams(vmem_limit_bytes=VMEM_LIMIT),
        name="logf_cumsum",
    )(xt)


def _flash_kernel(q_ref, k_ref, v_ref, ck_ref, cq_ref, o_ref, m_sc, l_sc, acc_sc, *, tq, tk):
    qi = pl.program_id(1)
    ki = pl.program_id(2)
    last = ((qi + 1) * tq - 1) // tk

    @pl.when(ki == 0)
    def _():
        m_sc[...] = jnp.full(m_sc.shape, -jnp.inf, F32)
        l_sc[...] = jnp.zeros(l_sc.shape, F32)
        acc_sc[...] = jnp.zeros(acc_sc.shape, F32)

    def step(masked):
        s = lax.dot_general(q_ref[...], k_ref[...], NT_DIMS, preferred_element_type=F32) * ATTN_SCALE
        s = s + (cq_ref[0, :, 0:1] - ck_ref[0])
        if masked:
            qpos = qi * tq + lax.broadcasted_iota(jnp.int32, (tq, tk), 0)
            kpos = ki * tk + lax.broadcasted_iota(jnp.int32, (tq, tk), 1)
            s = jnp.where(kpos <= qpos, s, NEG_INF)
        m_old = m_sc[...]
        m_new = jnp.maximum(m_old, jnp.max(s, axis=1, keepdims=True))
        a = jnp.exp(m_old - m_new)
        p = jnp.exp(s - m_new)
        l_sc[...] = a * l_sc[...] + jnp.sum(p, axis=1, keepdims=True)
        acc_sc[...] = a * acc_sc[...] + jnp.dot(p.astype(BF16), v_ref[...], preferred_element_type=F32)
        m_sc[...] = m_new

    first_masked = (qi * tq) // tk

    @pl.when(ki < first_masked)
    def _():
        step(False)

    @pl.when(jnp.logical_and(ki >= first_masked, ki <= last))
    def _():
        step(True)

    @pl.when(ki == last)
    def _():
        o_ref[...] = (acc_sc[...] / l_sc[...]).astype(o_ref.dtype)


def _flash_attention(qb, kb, vb, cst, *, tq, tk):
    n, d = qb.shape
    nh = d // HEAD_DIM
    cst3 = cst.reshape(nh, 1, n)
    last = lambda qi: ((qi + 1) * tq - 1) // tk
    kv_map = lambda h, qi, ki: (jnp.minimum(ki, last(qi)), h)
    return pl.pallas_call(
        functools.partial(_flash_kernel, tq=tq, tk=tk),
        grid=(nh, n // tq, n // tk),
        in_specs=[
            pl.BlockSpec((tq, HEAD_DIM), lambda h, qi, ki: (qi, h)),
            pl.BlockSpec((tk, HEAD_DIM), kv_map),
            pl.BlockSpec((tk, HEAD_DIM), kv_map),
            pl.BlockSpec((1, 1, tk), lambda h, qi, ki: (h, 0, jnp.minimum(ki, last(qi)))),
            pl.BlockSpec((1, 1, tq), lambda h, qi, ki: (h, 0, qi)),
        ],
        out_specs=pl.BlockSpec((tq, HEAD_DIM), lambda h, qi, ki: (qi, h)),
        out_shape=jax.ShapeDtypeStruct((n, d), BF16),
        scratch_shapes=[
            pltpu.VMEM((tq, 1), F32),
            pltpu.VMEM((tq, 1), F32),
            pltpu.VMEM((tq, HEAD_DIM), F32),
        ],
        compiler_params=_params(("parallel", "parallel", "arbitrary")),
        name="fox_flash",
    )(qb, kb, vb, cst3, cst3)


def _decode_kernel(pt_ref, q_ref, kn_ref, vn_ref, lfn_ref, k_ref, v_ref, lf_ref, o_ref,
                   m_sc, l_sc, acc_sc, c_sc):
    del pt_ref
    p = pl.program_id(1)
    nh, d = acc_sc.shape
    row = lax.broadcasted_iota(jnp.int32, (nh, d), 0)
    col = lax.broadcasted_iota(jnp.int32, (nh, d), 1)
    head_mask = (col // HEAD_DIM) == row
    q_rows = jnp.where(head_mask, q_ref[0].astype(F32), 0.0)

    @pl.when(p == 0)
    def _():
        m_sc[...] = jnp.full(m_sc.shape, -jnp.inf, F32)
        l_sc[...] = jnp.zeros(l_sc.shape, F32)
        acc_sc[...] = jnp.zeros(acc_sc.shape, F32)
        c_sc[...] = jnp.zeros(c_sc.shape, F32)

    def update(s, pv_of):
        m_old = m_sc[...]
        m_new = jnp.maximum(m_old, jnp.max(s, axis=1, keepdims=True))
        a = jnp.exp(m_old - m_new)
        pe = jnp.exp(s - m_new)
        l_sc[...] = a * l_sc[...] + jnp.sum(pe, axis=1, keepdims=True)
        acc_sc[...] = a * acc_sc[...] + pv_of(pe)
        m_sc[...] = m_new

    s = lax.dot_general(q_rows.astype(BF16), k_ref[...].astype(BF16), NT_DIMS,
                        preferred_element_type=F32) * ATTN_SCALE
    cs = _cumsum_lanes(lf_ref[...]) + c_sc[...]
    c_sc[...] = cs[:, PAGE_SIZE - 1:PAGE_SIZE]
    update(s - cs, lambda pe: jnp.dot(pe.astype(BF16), v_ref[...].astype(BF16),
                                      preferred_element_type=F32))

    @pl.when(p == pl.num_programs(1) - 1)
    def _():
        kn = kn_ref[0].astype(F32)
        s_new = jnp.sum(q_rows * kn, axis=1, keepdims=True) * ATTN_SCALE
        s_new = s_new - (c_sc[...] + lfn_ref[0])
        update(s_new, lambda pe: pe.astype(BF16).astype(F32) * vn_ref[0].astype(F32))
        o = jnp.where(head_mask, acc_sc[...] / l_sc[...], 0.0)
        o_ref[0] = jnp.sum(o, axis=0, keepdims=True).astype(o_ref.dtype)


def _decode_attention(page_table, qb, kb_new, vb_new, logf_new, cache_k, cache_v, cache_lft, layer):
    nb, d = qb.shape
    nh = d // HEAD_DIM
    n_pages = page_table.shape[1]
    per_seq = lambda b, p, pt: (b, 0, 0)
    page = lambda b, p, pt: (layer, pt[b, p], 0, 0)
    grid_spec = pltpu.PrefetchScalarGridSpec(
        num_scalar_prefetch=1,
        grid=(nb, n_pages),
        in_specs=[
            pl.BlockSpec((1, 1, d), per_seq),
            pl.BlockSpec((1, 1, d), per_seq),
            pl.BlockSpec((1, 1, d), per_seq),
            pl.BlockSpec((1, nh, 1), per_seq),
            pl.BlockSpec((None, None, PAGE_SIZE, d), page),
            pl.BlockSpec((None, None, PAGE_SIZE, d), page),
            pl.BlockSpec((None, None, nh, PAGE_SIZE), page),
        ],
        out_specs=pl.BlockSpec((1, 1, d), per_seq),
        scratch_shapes=[
            pltpu.VMEM((nh, 1), F32),
            pltpu.VMEM((nh, 1), F32),
            pltpu.VMEM((nh, d), F32),
            pltpu.VMEM((nh, 1), F32),
        ],
    )
    out = pl.pallas_call(
        _decode_kernel,
        grid_spec=grid_spec,
        out_shape=jax.ShapeDtypeStruct((nb, 1, d), BF16),
        compiler_params=_params(("parallel", "arbitrary")),
        name="fox_decode",
    )(page_table, qb[:, None, :], kb_new[:, None, :], vb_new[:, None, :], logf_new[:, :, None],
      cache_k, cache_v, cache_lft)
    return out[:, 0, :]


def _proj_ln_kernel(a_ref, w_ref, x_ref, g_ref, b_ref, o_ref, *, alpha):
    y = jnp.dot(a_ref[...], w_ref[...], preferred_element_type=F32)
    o_ref[...] = _layer_norm(alpha * x_ref[...] + y, g_ref[...], b_ref[...])


def _proj_ln(a, w, x, g, b, *, alpha, tm):
    n, d = x.shape
    rows = lambda i: (i, 0)
    fixed = lambda i: (0, 0)
    return pl.pallas_call(
        functools.partial(_proj_ln_kernel, alpha=alpha),
        grid=(n // tm,),
        in_specs=[
            pl.BlockSpec((tm, a.shape[1]), rows),
            pl.BlockSpec(w.shape, fixed),
            pl.BlockSpec((tm, d), rows),
            pl.BlockSpec((1, d), fixed),
            pl.BlockSpec((1, d), fixed),
        ],
        out_specs=pl.BlockSpec((tm, d), rows),
        out_shape=jax.ShapeDtypeStruct((n, d), F32),
        compiler_params=_params(("parallel",)),
        name="proj_ln",
    )(a, w, x, g, b)


HALO = POOL_STATE + 1


def _pool_prompt_kernel(x_ref, w_ref, sc_ref, g_ref, b_ref, o_ref, buf, *, tm, alpha):
    i = pl.program_id(0)
    gd = w_ref.shape[1]

    @pl.when(i == 0)
    def _():
        buf[0:HALO, :] = jnp.zeros((HALO, buf.shape[1]), F32)

    @pl.when(i > 0)
    def _():
        buf[0:HALO, :] = buf[tm:tm + HALO, :]

    x = x_ref[...]
    buf[HALO:HALO + tm, :] = x
    pos = i * tm + lax.broadcasted_iota(jnp.int32, (tm, 1), 0)
    ys = []
    for g, w in enumerate(POOL_WINDOWS):
        cols = slice(g * gd, (g + 1) * gd)
        win = x[:, cols]
        for back in range(1, w):
            win = win + buf[HALO - back:HALO - back + tm, cols]
        cnt = jnp.minimum(pos + 1, w).astype(F32)
        z = win / cnt - x[:, cols]
        ys.append(jnp.dot(z.astype(BF16), w_ref[g], preferred_element_type=F32))
    y = jnp.concatenate(ys, axis=1) * sc_ref[...]
    o_ref[...] = _layer_norm(alpha * x + y, g_ref[...], b_ref[...])


def _pool_prompt(x, w, scale, g, b, *, alpha, tm):
    n, d = x.shape
    rows = lambda i: (i, 0)
    fixed = lambda i: (0, 0)
    return pl.pallas_call(
        functools.partial(_pool_prompt_kernel, tm=tm, alpha=alpha),
        grid=(n // tm,),
        in_specs=[
            pl.BlockSpec((tm, d), rows),
            pl.BlockSpec(w.shape, lambda i: (0, 0, 0)),
            pl.BlockSpec((1, d), fixed),
            pl.BlockSpec((1, d), fixed),
            pl.BlockSpec((1, d), fixed),
        ],
        out_specs=pl.BlockSpec((tm, d), rows),
        out_shape=jax.ShapeDtypeStruct((n, d), F32),
        scratch_shapes=[pltpu.VMEM((HALO + tm, d), F32)],
        compiler_params=_params(("arbitrary",)),
        name="pool_prompt",
    )(x, w, scale, g, b)


def _pool_sample_kernel(xx_ref, w_ref, sc_ref, g_ref, b_ref, o_ref, *, pos0, alpha):
    gd = w_ref.shape[1]
    rows = xx_ref.shape[0]
    x = xx_ref[rows - 1]
    ys = []
    for g, w in enumerate(POOL_WINDOWS):
        cols = slice(g * gd, (g + 1) * gd)
        win = x[:, cols]
        for back in range(1, w):
            win = win + xx_ref[rows - 1 - back, :, cols]
        z = win / float(min(pos0 + 1, w)) - x[:, cols]
        ys.append(jnp.dot(z.astype(BF16), w_ref[g], preferred_element_type=F32))
    y = jnp.concatenate(ys, axis=1) * sc_ref[...]
    o_ref[...] = _layer_norm(alpha * x + y, g_ref[...], b_ref[...])


def _pool_sample(xx, w, scale, g, b, *, pos0, alpha):
    _, nb, d = xx.shape
    return pl.pallas_call(
        functools.partial(_pool_sample_kernel, pos0=pos0, alpha=alpha),
        out_shape=jax.ShapeDtypeStruct((nb, d), F32),
        compiler_params=pltpu.CompilerParams(vmem_limit_bytes=VMEM_LIMIT),
        name="pool_sample",
    )(xx, w, scale, g, b)


def _extract_top16(s, ids):
    big = float(s.shape[0])
    vals = []
    for _ in range(PEER_TOPK):
        m = jnp.max(s, axis=0, keepdims=True)
        first = jnp.min(jnp.where(s == m, ids, big), axis=0, keepdims=True)
        s = jnp.where(ids == first, -jnp.inf, s)
        vals.append(m)
    return vals


def _peer_stats_group(s1, s2):
    nk, ln = s1.shape
    ids = lax.broadcasted_iota(jnp.int32, (nk, ln), 0).astype(F32)
    t1 = _extract_top16(s1, ids)
    t2 = _extract_top16(s2, ids)
    row16 = lax.broadcasted_iota(jnp.int32, (PEER_TOPK, ln), 0)
    t2_arr = jnp.zeros((PEER_TOPK, ln), F32)
    for r in range(PEER_TOPK):
        t2_arr = jnp.where(row16 == r, t2[r], t2_arr)
    row8 = lax.broadcasted_iota(jnp.int32, (SUBLANES, ln), 0)
    pieces = [t1[0] + t2_arr]
    for i in range(1, PEER_TOPK):
        pieces.append(jnp.where(row8 < PEER_TOPK // (i + 1), t1[i] + t2_arr[0:SUBLANES], -jnp.inf))
    cand = jnp.concatenate(pieces, axis=0)
    cids = lax.broadcasted_iota(jnp.int32, cand.shape, 0).astype(F32)
    top = _extract_top16(cand, cids)
    mx, tau = top[0], top[PEER_TOPK - 1]
    z = jnp.sum(jnp.where(cand >= tau, jnp.exp(cand - mx), 0.0), axis=0, keepdims=True)
    w1 = jnp.exp(s1 - t1[0]) / z
    e2 = jnp.exp(s2 - t2[0])
    return w1, e2, tau


def _peer_stats_kernel(x_ref, wqt_ref, k1_ref, k2_ref, s1_ref, s2_ref, w1_ref, e2_ref, tau_ref, qt_sc):
    h = pl.program_id(1)

    @pl.when(h == 0)
    def _():
        qt = lax.dot_general(wqt_ref[...], x_ref[...].astype(BF16), NT_DIMS, preferred_element_type=F32)
        qt_sc[...] = qt.astype(BF16)

    base = pl.multiple_of(h * PEER_QDIM, PEER_QDIM)
    q1 = qt_sc[pl.ds(base, PEER_QHALF), :]
    q2 = qt_sc[pl.ds(base + PEER_QHALF, PEER_QHALF), :]
    s1 = jnp.dot(k1_ref[0], q1, preferred_element_type=F32)
    s2 = jnp.dot(k2_ref[0], q2, preferred_element_type=F32)
    s1_ref[0] = s1
    s2_ref[0] = s2
    for grp in range(s1.shape[1] // LANES):
        lanes = slice(grp * LANES, (grp + 1) * LANES)
        w1, e2, tau = _peer_stats_group(s1[:, lanes], s2[:, lanes])
        w1_ref[0, :, lanes] = w1
        e2_ref[0, :, lanes] = e2
        tau_ref[0, :, lanes] = tau


def _peer_stats(x, wqt, k1, k2, *, tm):
    n, d = x.shape
    nh, nk, _ = k1.shape
    per_head = lambda i, h: (h, 0, i)
    keys = lambda i, h: (h, 0, 0)
    full = jax.ShapeDtypeStruct((nh, nk, n), F32)
    return pl.pallas_call(
        _peer_stats_kernel,
        grid=(n // tm, nh),
        in_specs=[
            pl.BlockSpec((tm, d), lambda i, h: (i, 0)),
            pl.BlockSpec(wqt.shape, lambda i, h: (0, 0)),
            pl.BlockSpec((1, nk, k1.shape[2]), keys),
            pl.BlockSpec((1, nk, k2.shape[2]), keys),
        ],
        out_specs=[
            pl.BlockSpec((1, nk, tm), per_head),
            pl.BlockSpec((1, nk, tm), per_head),
            pl.BlockSpec((1, nk, tm), per_head),
            pl.BlockSpec((1, nk, tm), per_head),
            pl.BlockSpec((1, 1, tm), per_head),
        ],
        out_shape=[full, full, full, full, jax.ShapeDtypeStruct((nh, 1, n), F32)],
        scratch_shapes=[pltpu.VMEM((wqt.shape[0], tm), BF16)],
        compiler_params=_params(("parallel", "arbitrary")),
        name="peer_stats",
    )(x, wqt, k1, k2)


def _gelu(h):
    return 0.5 * h * (1.0 + lax.erf(h * (2.0 ** -0.5)))


def _peer_main_kernel(x_ref, u_ref, vt_ref, s1_ref, w1_ref, s2_ref, e2_ref, tau_ref, g_ref, b_ref,
                      o_ref, xb_sc, acc_sc, *, alpha):
    j = pl.program_id(1)
    nh, nk, _ = s2_ref.shape
    chunk = u_ref.shape[0] // nk

    @pl.when(j == 0)
    def _():
        xb_sc[...] = x_ref[...].astype(BF16)
        acc_sc[...] = jnp.zeros(acc_sc.shape, F32)

    hid = lax.dot_general(u_ref[...], xb_sc[...], NT_DIMS, preferred_element_type=F32)
    acts = []
    for aa in range(chunk):
        a = j * chunk + aa
        gate = None
        for h in range(nh):
            s1_row = s1_ref[h, pl.ds(a, 1), :]
            w1_row = w1_ref[h, pl.ds(a, 1), :]
            sel = (s1_row + s2_ref[h]) >= tau_ref[h]
            term = jnp.where(sel, w1_row * e2_ref[h], 0.0)
            gate = term if gate is None else gate + term
        acts.append((_gelu(hid[aa * nk:(aa + 1) * nk, :]) * gate).astype(BF16))
    act = jnp.concatenate(acts, axis=0)
    acc_sc[...] += jnp.dot(vt_ref[...], act, preferred_element_type=F32)

    @pl.when(j == pl.num_programs(1) - 1)
    def _():
        y = acc_sc[...].T
        o_ref[...] = _layer_norm(alpha * x_ref[...] + y, g_ref[...], b_ref[...])


def _peer_main(x, u, vt, s1, w1, s2, e2, tau, g, b, *, alpha, tm, te):
    n, d = x.shape
    ne = u.shape[0]
    nh, nk, _ = s1.shape
    tok = lambda i, j: (0, 0, i)
    fixed = lambda i, j: (0, 0)
    return pl.pallas_call(
        functools.partial(_peer_main_kernel, alpha=alpha),
        grid=(n // tm, ne // te),
        in_specs=[
            pl.BlockSpec((tm, d), lambda i, j: (i, 0)),
            pl.BlockSpec((te, d), lambda i, j: (j, 0)),
            pl.BlockSpec((d, te), lambda i, j: (0, j)),
            pl.BlockSpec((nh, nk, tm), tok),
            pl.BlockSpec((nh, nk, tm), tok),
            pl.BlockSpec((nh, nk, tm), tok),
            pl.BlockSpec((nh, nk, tm), tok),
            pl.BlockSpec((nh, 1, tm), tok),
            pl.BlockSpec((1, d), fixed),
            pl.BlockSpec((1, d), fixed),
        ],
        out_specs=pl.BlockSpec((tm, d), lambda i, j: (i, 0)),
        out_shape=jax.ShapeDtypeStruct((n, d), F32),
        scratch_shapes=[pltpu.VMEM((tm, d), BF16), pltpu.VMEM((d, tm), F32)],
        compiler_params=_params(("parallel", "arbitrary")),
        name="peer_main",
    )(x, u, vt, s1, w1, s2, e2, tau, g, b)


def _peer_ffn_ln(x, wqt, k1, k2, u, vt, g, b, *, alpha, tm, te):
    s1, s2, w1, e2, tau = _peer_stats(x, wqt, k1, k2, tm=tm)
    return _peer_main(x, u, vt, s1, w1, s2, e2, tau, g, b, alpha=alpha, tm=tm, te=te)


PROMPT_TM = 512
PEER_TE = 512
SAMPLE_PAD = LANES


def kernel(x_prompt, x_sample, cache_k, cache_v, cache_logf, page_table, state_pool, attn_w_in, attn_b_f,
           attn_w_o, pool_w, pool_scale, peer_w_q, peer_keys1, peer_keys2, peer_u, peer_v, ln_mix_g,
           ln_mix_b, ln_ffn_g, ln_ffn_b):
    bsz, seq, d = x_prompt.shape
    nb, dec_seq, _ = x_sample.shape
    assert bsz == 1 and dec_seq == 1
    depth = ln_mix_g.shape[0]
    alpha = (2.0 * depth) ** 0.25
    nh = d // HEAD_DIM
    past_len = page_table.shape[1] * PAGE_SIZE
    n_attn, n_phys = cache_k.shape[0], cache_k.shape[1]

    xp = x_prompt[0]
    xs = x_sample[:, 0, :]
    row = lambda v: v.reshape(1, -1)

    def peer(x, layer, tm):
        wqt = peer_w_q[layer].T.astype(BF16)
        u = peer_u[layer].astype(BF16)
        vt = peer_v[layer].T.astype(BF16)
        return _peer_ffn_ln(x, wqt, peer_keys1[layer].astype(BF16), peer_keys2[layer].astype(BF16),
                            u, vt, row(ln_ffn_g[layer]), row(ln_ffn_b[layer]),
                            alpha=alpha, tm=tm, te=PEER_TE)

    def peer_sample(x, layer):
        xpad = jnp.pad(x, ((0, SAMPLE_PAD - nb), (0, 0)))
        return peer(xpad, layer, SAMPLE_PAD)[:nb]

    kp_l, vp_l, lp_l, ks_l, vs_l, ls_l, pp_l, ps_l = [], [], [], [], [], [], [], []
    for i in range(depth):
        j = i // 2
        g_mix, b_mix = row(ln_mix_g[i]), row(ln_mix_b[i])
        if i % 2 == 0:
            w_in = attn_w_in[j]
            wq = w_in[:, :d].astype(BF16)
            wk = w_in[:, d:2 * d].astype(BF16)
            wv = w_in[:, 2 * d:3 * d].astype(BF16)
            wf = w_in[:, 3 * d:].astype(BF16)
            b_f = attn_b_f[j]
            proj = functools.partial(_fox_proj, wq=wq, wk=wk, wv=wv, wf=wf, wft=wf.T,
                                     b_f=b_f.reshape(1, nh), b_ft=b_f.reshape(nh, 1))
            wo = attn_w_o[j].astype(BF16)

            qb, k, kb, v, vb, lf, lft = proj(xp, tm=PROMPT_TM, tn=512)
            cst = _cumsum_rows_of(lft)
            o = _flash_attention(qb, kb, vb, cst, tq=PROMPT_TM, tk=PROMPT_TM)
            xp = _proj_ln(o, wo, xp, g_mix, b_mix, alpha=alpha, tm=256)
            kp_l.append(k.reshape(1, seq, nh, HEAD_DIM))
            vp_l.append(v.reshape(1, seq, nh, HEAD_DIM))
            lp_l.append(lf.reshape(1, seq, nh))

            qb, k, kb, v, vb, lf, lft = proj(xs, tm=nb, tn=512)
            ck = cache_k.reshape(n_attn, n_phys, PAGE_SIZE, d)
            cv = cache_v.reshape(n_attn, n_phys, PAGE_SIZE, d)
            clt = jnp.swapaxes(cache_logf, 2, 3)
            o = _decode_attention(page_table, qb, kb, vb, lf, ck, cv, clt, j)
            xs = _proj_ln(o, wo, xs, g_mix, b_mix, alpha=alpha, tm=nb)
            ks_l.append(k.reshape(nb, 1, nh, HEAD_DIM))
            vs_l.append(v.reshape(nb, 1, nh, HEAD_DIM))
            ls_l.append(lf.reshape(nb, 1, nh))
        else:
            pw = pool_w[j].astype(BF16)
            psc = row(pool_scale[j])
            prefix = jnp.zeros((POOL_STATE, d), xp.dtype)
            pp_l.append(jnp.concatenate([prefix, xp], axis=0)[-POOL_STATE:][None])
            xx = jnp.concatenate([state_pool[j], xs[:, None, :]], axis=1)
            ps_l.append(xx[:, -POOL_STATE:])
            xp = _pool_prompt(xp, pw, psc, g_mix, b_mix, alpha=alpha, tm=PROMPT_TM)
            xs = _pool_sample(jnp.swapaxes(xx, 0, 1), pw, psc, g_mix, b_mix, pos0=past_len, alpha=alpha)
        xp = peer(xp, i, PROMPT_TM)
        xs = peer_sample(xs, i)
    return (xp[None], xs[:, None, :], jnp.stack(kp_l), jnp.stack(vp_l), jnp.stack(lp_l),
            jnp.stack(ks_l), jnp.stack(vs_l), jnp.stack(ls_l), jnp.stack(pp_l), jnp.stack(ps_l))
```

```python
import functools

import jax
import jax.numpy as jnp
from jax import lax
from jax.experimental import pallas as pl
from jax.experimental.pallas import tpu as pltpu

BF16 = jnp.bfloat16
F32 = jnp.float32

HEAD_DIM = 128
PAGE_SIZE = 128
ATTN_SCALE = HEAD_DIM ** -0.5
NEG_INF = -1e30
POOL_WINDOWS = (2, 4, 8, 16)
POOL_STATE = max(POOL_WINDOWS) - 1
PEER_HEADS = 8
PEER_NKEYS = 128
PEER_TOPK = 16
PEER_QDIM = 256
PEER_QHALF = PEER_QDIM // 2
LN_EPS = 1e-5

LANES = 128
SUBLANES = 8
V7X_VMEM_BYTES = 64 * 1024 * 1024
VMEM_LIMIT = V7X_VMEM_BYTES * 7 // 8

NT_DIMS = (((1,), (1,)), ((), ()))

SKIP_LOGIT = -100.0


def _params(semantics):
    return pltpu.CompilerParams(dimension_semantics=semantics, vmem_limit_bytes=VMEM_LIMIT)


def _layer_norm(y, g, b):
    mu = jnp.mean(y, axis=-1, keepdims=True)
    d = y - mu
    var = jnp.mean(d * d, axis=-1, keepdims=True)
    return d * lax.rsqrt(var + LN_EPS) * g + b


def _log_sigmoid(z):
    return jnp.minimum(z, 0.0) - jnp.log1p(jnp.exp(-jnp.abs(z)))


def _cumsum_lanes(c, first_shift=1):
    n = c.shape[-1]
    lane = lax.broadcasted_iota(jnp.int32, c.shape, c.ndim - 1)
    s = first_shift
    while s < n:
        c = c + jnp.where(lane >= s, pltpu.roll(c, s, axis=c.ndim - 1), 0.0)
        s *= 2
    return c


def _fox_proj_kernel(x_ref, wq_ref, wk_ref, wv_ref, wf_ref, wft_ref, bf_ref, bft_ref,
                     qb_ref, k_ref, kb_ref, v_ref, vb_ref, lf_ref, lft_ref, qn2_ref, kn2_ref):
    xb = x_ref[...].astype(BF16)
    tn, heads = wq_ref.shape[1], qn2_ref.shape[2]
    seg = (lax.broadcasted_iota(jnp.int32, (tn, heads), 0) // HEAD_DIM
           == lax.broadcasted_iota(jnp.int32, (tn, heads), 1)).astype(BF16)
    qb = jnp.dot(xb, wq_ref[...], preferred_element_type=F32).astype(BF16)
    qb_ref[...] = qb
    qn2_ref[0] = jnp.dot(qb * qb, seg, preferred_element_type=F32)
    k = jnp.dot(xb, wk_ref[...], preferred_element_type=F32)
    k_ref[...] = k
    kb = k.astype(BF16)
    kb_ref[...] = kb
    kn2_ref[0] = jnp.dot(kb * kb, seg, preferred_element_type=F32)
    v = jnp.dot(xb, wv_ref[...], preferred_element_type=F32)
    v_ref[...] = v
    vb_ref[...] = v.astype(BF16)

    @pl.when(pl.program_id(1) == 0)
    def _():
        z = jnp.dot(xb, wf_ref[...], preferred_element_type=F32) + bf_ref[...]
        lf_ref[...] = _log_sigmoid(z)
        zt = lax.dot_general(wft_ref[...], xb, NT_DIMS, preferred_element_type=F32) + bft_ref[...]
        lft_ref[...] = _log_sigmoid(zt)


def _fox_proj(x, wq, wk, wv, wf, wft, b_f, b_ft, *, tm, tn):
    n, d = x.shape
    h = wf.shape[1]
    hpt = tn // HEAD_DIM
    row_col = lambda i, j: (i, j)
    norms = jax.ShapeDtypeStruct((d // tn, n, hpt), F32)
    norm_spec = pl.BlockSpec((1, tm, hpt), lambda i, j: (j, i, 0))
    return pl.pallas_call(
        _fox_proj_kernel,
        grid=(n // tm, d // tn),
        in_specs=[
            pl.BlockSpec((tm, d), lambda i, j: (i, 0)),
            pl.BlockSpec((d, tn), lambda i, j: (0, j)),
            pl.BlockSpec((d, tn), lambda i, j: (0, j)),
            pl.BlockSpec((d, tn), lambda i, j: (0, j)),
            pl.BlockSpec((d, h), lambda i, j: (0, 0)),
            pl.BlockSpec((h, d), lambda i, j: (0, 0)),
            pl.BlockSpec((1, h), lambda i, j: (0, 0)),
            pl.BlockSpec((h, 1), lambda i, j: (0, 0)),
        ],
        out_specs=[
            pl.BlockSpec((tm, tn), row_col),
            pl.BlockSpec((tm, tn), row_col),
            pl.BlockSpec((tm, tn), row_col),
            pl.BlockSpec((tm, tn), row_col),
            pl.BlockSpec((tm, tn), row_col),
            pl.BlockSpec((tm, h), lambda i, j: (i, 0)),
            pl.BlockSpec((h, tm), lambda i, j: (0, i)),
            norm_spec,
            norm_spec,
        ],
        out_shape=[
            jax.ShapeDtypeStruct((n, d), BF16),
            jax.ShapeDtypeStruct((n, d), F32),
            jax.ShapeDtypeStruct((n, d), BF16),
            jax.ShapeDtypeStruct((n, d), F32),
            jax.ShapeDtypeStruct((n, d), BF16),
            jax.ShapeDtypeStruct((n, h), F32),
            jax.ShapeDtypeStruct((h, n), F32),
            norms,
            norms,
        ],
        compiler_params=_params(("parallel", "arbitrary")),
        name="fox_proj",
    )(x, wq, wk, wv, wf, wft, b_f, b_ft)


def _cumsum_kernel(x_ref, o_ref):
    o_ref[...] = _cumsum_lanes(x_ref[...])


def _cumsum_rows_of(xt):
    return pl.pallas_call(
        _cumsum_kernel,
        out_shape=jax.ShapeDtypeStruct(xt.shape, F32),
        compiler_params=pltpu.CompilerParams(vmem_limit_bytes=VMEM_LIMIT),
        name="logf_cumsum",
    )(xt)


def _flash_kernel(need_ref, q_ref, k_ref, v_ref, c_ref, o_ref, m_sc, l_sc, acc_sc, *, blk):
    h = pl.program_id(0)
    qi = pl.program_id(1)
    m_sc[...] = jnp.full(m_sc.shape, -jnp.inf, F32)
    l_sc[...] = jnp.zeros(l_sc.shape, F32)
    acc_sc[...] = jnp.zeros(acc_sc.shape, F32)
    q = q_ref[...]
    q0 = pl.multiple_of(qi * blk, blk)
    c_q0 = c_ref[0, :, pl.ds(q0, LANES)][:, 0:1]

    def block(kj, diagonal):
        k0 = pl.multiple_of(kj * blk, blk)
        s = lax.dot_general(q, k_ref[pl.ds(k0, blk), :], NT_DIMS, preferred_element_type=F32) * ATTN_SCALE
        s = s + (c_q0 - c_ref[0, :, pl.ds(k0, blk)])
        if diagonal:
            rows = lax.broadcasted_iota(jnp.int32, (blk, blk), 0)
            cols = lax.broadcasted_iota(jnp.int32, (blk, blk), 1)
            s = jnp.where(cols <= rows, s, NEG_INF)
        m_old = m_sc[...]
        m_new = jnp.maximum(m_old, jnp.max(s, axis=1, keepdims=True))
        a = jnp.exp(m_old - m_new)
        p = jnp.exp(s - m_new)
        l_sc[...] = a * l_sc[...] + jnp.sum(p, axis=1, keepdims=True)
        acc_sc[...] = a * acc_sc[...] + jnp.dot(p.astype(BF16), v_ref[pl.ds(k0, blk), :],
                                                preferred_element_type=F32)
        m_sc[...] = m_new

    needed = need_ref[h * pl.num_programs(1) + qi]

    def body(kj, carry):
        @pl.when(((needed >> kj) & 1) != 0)
        def _():
            block(kj, False)
        return carry

    lax.fori_loop(0, qi, body, 0)
    block(qi, True)
    o_ref[...] = (acc_sc[...] / l_sc[...]).astype(o_ref.dtype)


def _flash_attention(need, qb, kb, vb, cst, *, blk):
    n, d = qb.shape
    nh = d // HEAD_DIM
    grid_spec = pltpu.PrefetchScalarGridSpec(
        num_scalar_prefetch=1,
        grid=(nh, n // blk),
        in_specs=[
            pl.BlockSpec((blk, HEAD_DIM), lambda h, qi, need: (qi, h)),
            pl.BlockSpec((n, HEAD_DIM), lambda h, qi, need: (0, h)),
            pl.BlockSpec((n, HEAD_DIM), lambda h, qi, need: (0, h)),
            pl.BlockSpec((1, 1, n), lambda h, qi, need: (h, 0, 0)),
        ],
        out_specs=pl.BlockSpec((blk, HEAD_DIM), lambda h, qi, need: (qi, h)),
        scratch_shapes=[
            pltpu.VMEM((blk, 1), F32),
            pltpu.VMEM((blk, 1), F32),
            pltpu.VMEM((blk, HEAD_DIM), F32),
        ],
    )
    return pl.pallas_call(
        functools.partial(_flash_kernel, blk=blk),
        grid_spec=grid_spec,
        out_shape=jax.ShapeDtypeStruct((n, d), BF16),
        compiler_params=_params(("parallel", "arbitrary")),
        name="fox_flash",
    )(need, qb, kb, vb, cst.reshape(nh, 1, n))


def _needed_key_blocks(qn2, kn2, cst, blk):
    nh, n = cst.shape
    nblk = n // blk
    assert nblk <= 32

    def block_norm(x2):
        x2 = x2.transpose(1, 0, 2).reshape(nblk, blk, nh)
        return jnp.sqrt(jnp.max(x2, axis=1)).T

    qn, kn = block_norm(qn2), block_norm(kn2)
    c_first = cst[:, ::blk]
    c_last = cst[:, blk - 1::blk]
    bound = (ATTN_SCALE * qn[:, :, None] * (kn[:, None, :] + kn[:, :, None])
             + c_first[:, :, None] - c_last[:, None, :])
    bits = (bound >= SKIP_LOGIT).astype(jnp.uint32) << jnp.arange(nblk, dtype=jnp.uint32)
    return lax.bitcast_convert_type(jnp.sum(bits, axis=2, dtype=jnp.uint32), jnp.int32).reshape(-1)


def _page_cumsum_kernel(x_ref, cs_ref, tot_ref, *, nh):
    x = x_ref[...]
    n = x.shape[1]
    cs = _cumsum_lanes(x, first_shift=nh)
    cs_ref[...] = cs
    lane = lax.broadcasted_iota(jnp.int32, x.shape, 1)
    t = jnp.where(lane >= n - nh, cs, 0.0)
    s = nh
    while s < n:
        t = t + pltpu.roll(t, s, axis=1)
        s *= 2
    tot_ref[...] = t


def _page_cumsum(lf, nh, *, rows):
    npg, span = lf.shape
    spec = pl.BlockSpec((rows, span), lambda i: (i, 0))
    shape = jax.ShapeDtypeStruct((npg, span), F32)
    return pl.pallas_call(
        functools.partial(_page_cumsum_kernel, nh=nh),
        grid=(npg // rows,),
        in_specs=[spec],
        out_specs=[spec, spec],
        out_shape=[shape, shape],
        compiler_params=_params(("parallel",)),
        name="page_cumsum",
    )(lf)


def _decode_kernel(pt_ref, q_ref, kn_ref, vn_ref, lfn_ref, *rest, ppb):
    del pt_ref
    k_refs, v_refs = rest[0:ppb], rest[ppb:2 * ppb]
    cs_refs, tot_refs = rest[2 * ppb:3 * ppb], rest[3 * ppb:4 * ppb]
    o_ref, m_sc, l_sc, acc_sc, c_sc = rest[4 * ppb:]
    g = pl.program_id(1)
    nh = acc_sc.shape[0]
    span = c_sc.shape[1]
    row = lax.broadcasted_iota(jnp.int32, (nh, span), 0)
    lane = lax.broadcasted_iota(jnp.int32, (nh, span), 1)
    own = (lane % nh) == row
    qm = q_ref[0]

    @pl.when(g == 0)
    def _():
        m_sc[...] = jnp.full(m_sc.shape, -jnp.inf, F32)
        l_sc[...] = jnp.zeros(l_sc.shape, F32)
        acc_sc[...] = jnp.zeros(acc_sc.shape, F32)
        c_sc[...] = jnp.zeros(c_sc.shape, F32)

    def update(s, pv_of):
        m_old = m_sc[...]
        m_new = jnp.maximum(m_old, jnp.max(s, axis=1, keepdims=True))
        a = jnp.exp(m_old - m_new)
        pe = jnp.exp(s - m_new)
        l_sc[...] = a * l_sc[...] + jnp.sum(pe, axis=1, keepdims=True)
        acc_sc[...] = a * acc_sc[...] + pv_of(pe)
        m_sc[...] = m_new

    for i in range(ppb):
        r = lax.dot_general(qm, k_refs[i][...].astype(BF16), NT_DIMS,
                            preferred_element_type=F32) * ATTN_SCALE
        bias = cs_refs[i][...] + c_sc[...]
        c_sc[...] += tot_refs[i][...]
        v_page = v_refs[i]
        update(jnp.where(own, r - bias, -jnp.inf),
               lambda pe: jnp.dot(pe.astype(BF16), v_page[...].astype(BF16), preferred_element_type=F32))

    @pl.when(g == pl.num_programs(1) - 1)
    def _():
        s_new = jnp.sum(qm.astype(F32) * kn_ref[0].astype(F32), axis=1, keepdims=True) * ATTN_SCALE
        c_col = jnp.sum(jnp.where(jnp.logical_and(own, lane < nh), c_sc[...], 0.0), axis=1, keepdims=True)
        s_new = s_new - (c_col + lfn_ref[0])
        update(s_new, lambda pe: pe.astype(BF16).astype(F32) * vn_ref[0].astype(F32))
        o_ref[0] = (acc_sc[...] / l_sc[...]).astype(o_ref.dtype)


def _decode_attention(page_table, qb, kb_new, vb_new, logf_new, cache_k, cache_v, cs, tot, layer, *, ppb):
    nb, nh, hd = qb.shape
    span = cache_k.shape[2]
    n_pages = page_table.shape[1]
    per_seq = lambda b, g, pt: (b, 0, 0)

    def page(i):
        return lambda b, g, pt: (layer, pt[b, g * ppb + i], 0, 0)

    kv_specs = [pl.BlockSpec((None, None, span, hd), page(i)) for i in range(ppb)]
    lf_specs = [pl.BlockSpec((None, None, 1, span), page(i)) for i in range(ppb)]
    grid_spec = pltpu.PrefetchScalarGridSpec(
        num_scalar_prefetch=1,
        grid=(nb, n_pages // ppb),
        in_specs=[
            pl.BlockSpec((1, nh, hd), per_seq),
            pl.BlockSpec((1, nh, hd), per_seq),
            pl.BlockSpec((1, nh, hd), per_seq),
            pl.BlockSpec((1, nh, 1), per_seq),
        ] + kv_specs + kv_specs + lf_specs + lf_specs,
        out_specs=pl.BlockSpec((1, nh, hd), per_seq),
        scratch_shapes=[
            pltpu.VMEM((nh, 1), F32),
            pltpu.VMEM((nh, 1), F32),
            pltpu.VMEM((nh, hd), F32),
            pltpu.VMEM((1, span), F32),
        ],
    )
    return pl.pallas_call(
        functools.partial(_decode_kernel, ppb=ppb),
        grid_spec=grid_spec,
        out_shape=jax.ShapeDtypeStruct((nb, nh, hd), BF16),
        compiler_params=_params(("parallel", "arbitrary")),
        name="fox_decode",
    )(page_table, qb, kb_new, vb_new, logf_new[:, :, None],
      *([cache_k] * ppb), *([cache_v] * ppb), *([cs] * ppb), *([tot] * ppb))


def _proj_ln_kernel(a_ref, w_ref, x_ref, g_ref, b_ref, o_ref, *, alpha):
    y = jnp.dot(a_ref[...], w_ref[...], preferred_element_type=F32)
    o_ref[...] = _layer_norm(alpha * x_ref[...] + y, g_ref[...], b_ref[...])


def _proj_ln(a, w, x, g, b, *, alpha, tm):
    n, d = x.shape
    rows = lambda i: (i, 0)
    fixed = lambda i: (0, 0)
    return pl.pallas_call(
        functools.partial(_proj_ln_kernel, alpha=alpha),
        grid=(n // tm,),
        in_specs=[
            pl.BlockSpec((tm, a.shape[1]), rows),
            pl.BlockSpec(w.shape, fixed),
            pl.BlockSpec((tm, d), rows),
            pl.BlockSpec((1, d), fixed),
            pl.BlockSpec((1, d), fixed),
        ],
        out_specs=pl.BlockSpec((tm, d), rows),
        out_shape=jax.ShapeDtypeStruct((n, d), F32),
        compiler_params=_params(("parallel",)),
        name="proj_ln",
    )(a, w, x, g, b)


HALO = POOL_STATE + 1


def _pool_prompt_kernel(x_ref, w_ref, sc_ref, g_ref, b_ref, o_ref, buf, *, tm, alpha):
    i = pl.program_id(0)
    gd = w_ref.shape[1]

    @pl.when(i == 0)
    def _():
        buf[0:HALO, :] = jnp.zeros((HALO, buf.shape[1]), F32)

    @pl.when(i > 0)
    def _():
        buf[0:HALO, :] = buf[tm:tm + HALO, :]

    x = x_ref[...]
    buf[HALO:HALO + tm, :] = x
    pos = i * tm + lax.broadcasted_iota(jnp.int32, (tm, 1), 0)
    ys = []
    for g, w in enumerate(POOL_WINDOWS):
        cols = slice(g * gd, (g + 1) * gd)
        win = x[:, cols]
        for back in range(1, w):
            win = win + buf[HALO - back:HALO - back + tm, cols]
        cnt = jnp.minimum(pos + 1, w).astype(F32)
        z = win / cnt - x[:, cols]
        ys.append(jnp.dot(z.astype(BF16), w_ref[g], preferred_element_type=F32))
    y = jnp.concatenate(ys, axis=1) * sc_ref[...]
    o_ref[...] = _layer_norm(alpha * x + y, g_ref[...], b_ref[...])


def _pool_prompt(x, w, scale, g, b, *, alpha, tm):
    n, d = x.shape
    rows = lambda i: (i, 0)
    fixed = lambda i: (0, 0)
    return pl.pallas_call(
        functools.partial(_pool_prompt_kernel, tm=tm, alpha=alpha),
        grid=(n // tm,),
        in_specs=[
            pl.BlockSpec((tm, d), rows),
            pl.BlockSpec(w.shape, lambda i: (0, 0, 0)),
            pl.BlockSpec((1, d), fixed),
            pl.BlockSpec((1, d), fixed),
            pl.BlockSpec((1, d), fixed),
        ],
        out_specs=pl.BlockSpec((tm, d), rows),
        out_shape=jax.ShapeDtypeStruct((n, d), F32),
        scratch_shapes=[pltpu.VMEM((HALO + tm, d), F32)],
        compiler_params=_params(("arbitrary",)),
        name="pool_prompt",
    )(x, w, scale, g, b)


def _pool_sample_kernel(xx_ref, w_ref, sc_ref, g_ref, b_ref, o_ref, *, pos0, alpha):
    gd = w_ref.shape[1]
    rows = xx_ref.shape[0]
    x = xx_ref[rows - 1]
    ys = []
    for g, w in enumerate(POOL_WINDOWS):
        cols = slice(g * gd, (g + 1) * gd)
        win = x[:, cols]
        for back in range(1, w):
            win = win + xx_ref[rows - 1 - back, :, cols]
        z = win / float(min(pos0 + 1, w)) - x[:, cols]
        ys.append(jnp.dot(z.astype(BF16), w_ref[g], preferred_element_type=F32))
    y = jnp.concatenate(ys, axis=1) * sc_ref[...]
    o_ref[...] = _layer_norm(alpha * x + y, g_ref[...], b_ref[...])


def _pool_sample(xx, w, scale, g, b, *, pos0, alpha):
    _, nb, d = xx.shape
    return pl.pallas_call(
        functools.partial(_pool_sample_kernel, pos0=pos0, alpha=alpha),
        out_shape=jax.ShapeDtypeStruct((nb, d), F32),
        compiler_params=pltpu.CompilerParams(vmem_limit_bytes=VMEM_LIMIT),
        name="pool_sample",
    )(xx, w, scale, g, b)


def _oddeven_merge(lo, hi, r):
    step = r * 2
    if step < hi - lo:
        yield from _oddeven_merge(lo, hi, step)
        yield from _oddeven_merge(lo + r, hi, step)
        yield from [(i, i + r) for i in range(lo + r, hi - r, step)]
    else:
        yield (lo, lo + r)


def _oddeven_merge_sort(lo, hi):
    if hi - lo >= 1:
        mid = lo + (hi - lo) // 2
        yield from _oddeven_merge_sort(lo, mid)
        yield from _oddeven_merge_sort(mid + 1, hi)
        yield from _oddeven_merge(lo, hi, 1)


SORT_NETWORK = tuple(_oddeven_merge_sort(0, PEER_TOPK - 1))


def _sort_desc(xs):
    xs = list(xs)
    for i, j in SORT_NETWORK:
        a, b = xs[i], xs[j]
        if b is None:
            continue
        if a is None:
            xs[i], xs[j] = b, None
        else:
            xs[i], xs[j] = jnp.maximum(a, b), jnp.minimum(a, b)
    return xs


def _merge_sublanes(xs):
    k = len(xs)
    for shift in (4, 2, 1):
        ys = []
        for i in range(k):
            mine, theirs = xs[i], xs[k - 1 - i]
            if theirs is None:
                ys.append(mine)
            elif mine is None:
                ys.append(pltpu.roll(theirs, shift, axis=0))
            else:
                ys.append(jnp.maximum(mine, pltpu.roll(theirs, shift, axis=0)))
        stride = k // 2
        while stride >= 1:
            for i in range(k):
                if i & stride == 0:
                    a, b = ys[i], ys[i + stride]
                    ys[i], ys[i + stride] = jnp.maximum(a, b), jnp.minimum(a, b)
            stride //= 2
        xs = ys
    return xs


def _top16_rows(s):
    return _merge_sublanes(_sort_desc([s[SUBLANES * j:SUBLANES * (j + 1), :] for j in range(PEER_TOPK)]))


def _peer_stats_group(s1, s2):
    ln = s1.shape[1]
    t1 = _top16_rows(s1)
    t2 = _top16_rows(s2)
    sub = lax.broadcasted_iota(jnp.int32, (SUBLANES, ln), 0)

    def spread(ts):
        out = ts[0]
        for r in range(1, SUBLANES):
            out = jnp.where(sub == r, ts[r], out)
        return out

    t2_lo, t2_hi, t1_hi = spread(t2[:SUBLANES]), spread(t2[SUBLANES:]), spread(t1[SUBLANES:])
    cands = [t1[0] + t2_lo, t1[0] + t2_hi, t1[1] + t2_lo]
    for i in range(2, SUBLANES):
        cands.append(jnp.where(sub < PEER_TOPK // (i + 1), t1[i] + t2_lo, -jnp.inf))
    cands.append(t1_hi + t2[0])
    top = _merge_sublanes(_sort_desc(cands + [None] * (PEER_TOPK - len(cands))))
    mx, tau = top[0], top[PEER_TOPK - 1]
    z = jnp.where(cands[0] >= tau, jnp.exp(cands[0] - mx), 0.0)
    for c in cands[1:]:
        z = z + jnp.where(c >= tau, jnp.exp(c - mx), 0.0)
    z = jnp.sum(z, axis=0, keepdims=True)
    w1 = jnp.exp(s1 - t1[0][0:1]) / z
    e2 = jnp.exp(s2 - t2[0][0:1])
    return w1, e2, tau[0:1]


def _peer_stats_kernel(x_ref, wqt_ref, k1_ref, k2_ref, s1_ref, s2_ref, w1_ref, e2_ref, tau_ref, qt_sc):
    h = pl.program_id(1)

    @pl.when(h == 0)
    def _():
        qt = lax.dot_general(wqt_ref[...], x_ref[...].astype(BF16), NT_DIMS, preferred_element_type=F32)
        qt_sc[...] = qt.astype(BF16)

    base = pl.multiple_of(h * PEER_QDIM, PEER_QDIM)
    q1 = qt_sc[pl.ds(base, PEER_QHALF), :]
    q2 = qt_sc[pl.ds(base + PEER_QHALF, PEER_QHALF), :]
    s1_ref[0] = jnp.dot(k1_ref[0], q1, preferred_element_type=F32)
    s2_ref[0] = jnp.dot(k2_ref[0], q2, preferred_element_type=F32)
    for grp in range(s1_ref.shape[2] // LANES):
        lanes = slice(grp * LANES, (grp + 1) * LANES)
        w1, e2, tau = _peer_stats_group(s1_ref[0, :, lanes], s2_ref[0, :, lanes])
        w1_ref[0, :, lanes] = w1
        e2_ref[0, :, lanes] = e2
        tau_ref[0, :, lanes] = tau


def _peer_stats(x, wqt, k1, k2, *, tm):
    n, d = x.shape
    nh, nk, _ = k1.shape
    assert nk == PEER_TOPK * SUBLANES
    per_head = lambda i, h: (h, 0, i)
    keys = lambda i, h: (h, 0, 0)
    full = jax.ShapeDtypeStruct((nh, nk, n), F32)
    return pl.pallas_call(
        _peer_stats_kernel,
        grid=(n // tm, nh),
        in_specs=[
            pl.BlockSpec((tm, d), lambda i, h: (i, 0)),
            pl.BlockSpec(wqt.shape, lambda i, h: (0, 0)),
            pl.BlockSpec((1, nk, k1.shape[2]), keys),
            pl.BlockSpec((1, nk, k2.shape[2]), keys),
        ],
        out_specs=[
            pl.BlockSpec((1, nk, tm), per_head),
            pl.BlockSpec((1, nk, tm), per_head),
            pl.BlockSpec((1, nk, tm), per_head),
            pl.BlockSpec((1, nk, tm), per_head),
            pl.BlockSpec((1, 1, tm), per_head),
        ],
        out_shape=[full, full, full, full, jax.ShapeDtypeStruct((nh, 1, n), F32)],
        scratch_shapes=[pltpu.VMEM((wqt.shape[0], tm), BF16)],
        compiler_params=_params(("parallel", "arbitrary")),
        name="peer_stats",
    )(x, wqt, k1, k2)


def _gelu(h):
    return 0.5 * h * (1.0 + lax.erf(h * (2.0 ** -0.5)))


def _peer_main_kernel(x_ref, u_ref, vt_ref, s1_ref, w1_ref, s2_ref, e2_ref, tau_ref, g_ref, b_ref,
                      o_ref, xb_sc, hid_sc, act_sc, acc_sc, *, alpha):
    j = pl.program_id(1)
    nh, nk, tm = s2_ref.shape
    chunk = u_ref.shape[0] // nk

    @pl.when(j == 0)
    def _():
        xb_sc[...] = x_ref[...].astype(BF16)
        acc_sc[...] = jnp.zeros(acc_sc.shape, F32)

    hid_sc[...] = lax.dot_general(u_ref[...], xb_sc[...], NT_DIMS, preferred_element_type=F32)
    for aa in range(chunk):
        rows = slice(aa * nk, (aa + 1) * nk)
        for grp in range(tm // LANES):
            lanes = slice(grp * LANES, (grp + 1) * LANES)
            gate = None
            for h in range(nh):
                s1_row = s1_ref[h, aa:aa + 1, lanes]
                w1_row = w1_ref[h, aa:aa + 1, lanes]
                sel = (s1_row + s2_ref[h, :, lanes]) >= tau_ref[h, :, lanes]
                term = jnp.where(sel, w1_row * e2_ref[h, :, lanes], 0.0)
                gate = term if gate is None else gate + term
            act_sc[rows, lanes] = (_gelu(hid_sc[rows, lanes]) * gate).astype(BF16)
    acc_sc[...] += jnp.dot(vt_ref[...], act_sc[...], preferred_element_type=F32)

    @pl.when(j == pl.num_programs(1) - 1)
    def _():
        y = acc_sc[...].T
        o_ref[...] = _layer_norm(alpha * x_ref[...] + y, g_ref[...], b_ref[...])


def _peer_main(x, u, vt, s1, w1, s2, e2, tau, g, b, *, alpha, tm, te):
    n, d = x.shape
    ne = u.shape[0]
    nh, nk, _ = s1.shape
    chunk = te // nk
    tok = lambda i, j: (0, 0, i)
    first_key = lambda i, j: (0, j, i)
    fixed = lambda i, j: (0, 0)
    return pl.pallas_call(
        functools.partial(_peer_main_kernel, alpha=alpha),
        grid=(n // tm, ne // te),
        in_specs=[
            pl.BlockSpec((tm, d), lambda i, j: (i, 0), pipeline_mode=pl.Buffered(1)),
            pl.BlockSpec((te, d), lambda i, j: (j, 0)),
            pl.BlockSpec((d, te), lambda i, j: (0, j)),
            pl.BlockSpec((nh, chunk, tm), first_key),
            pl.BlockSpec((nh, chunk, tm), first_key),
            pl.BlockSpec((nh, nk, tm), tok),
            pl.BlockSpec((nh, nk, tm), tok),
            pl.BlockSpec((nh, 1, tm), tok),
            pl.BlockSpec((1, d), fixed),
            pl.BlockSpec((1, d), fixed),
        ],
        out_specs=pl.BlockSpec((tm, d), lambda i, j: (i, 0)),
        out_shape=jax.ShapeDtypeStruct((n, d), F32),
        scratch_shapes=[
            pltpu.VMEM((tm, d), BF16),
            pltpu.VMEM((te, tm), F32),
            pltpu.VMEM((te, tm), BF16),
            pltpu.VMEM((d, tm), F32),
        ],
        compiler_params=_params(("parallel", "arbitrary")),
        name="peer_main",
    )(x, u, vt, s1, w1, s2, e2, tau, g, b)


def _peer_ffn_ln(x, wqt, k1, k2, u, vt, g, b, *, alpha, tm, te):
    s1, s2, w1, e2, tau = _peer_stats(x, wqt, k1, k2, tm=tm)
    return _peer_main(x, u, vt, s1, w1, s2, e2, tau, g, b, alpha=alpha, tm=tm, te=te)


PROMPT_TM = 512
FLASH_BLOCK = 256
PEER_TE = 1024
SAMPLE_PAD = LANES
DECODE_PAGES = 4
PAGE_CUMSUM_ROWS = 256


def kernel(x_prompt, x_sample, cache_k, cache_v, cache_logf, page_table, state_pool, attn_w_in, attn_b_f,
           attn_w_o, pool_w, pool_scale, peer_w_q, peer_keys1, peer_keys2, peer_u, peer_v, ln_mix_g,
           ln_mix_b, ln_ffn_g, ln_ffn_b):
    bsz, seq, d = x_prompt.shape
    nb, dec_seq, _ = x_sample.shape
    assert bsz == 1 and dec_seq == 1
    depth = ln_mix_g.shape[0]
    alpha = (2.0 * depth) ** 0.25
    nh = d // HEAD_DIM
    past_len = page_table.shape[1] * PAGE_SIZE
    n_attn, n_phys = cache_k.shape[0], cache_k.shape[1]
    span = PAGE_SIZE * nh

    xp = x_prompt[0]
    xs = x_sample[:, 0, :]
    row = lambda v: v.reshape(1, -1)

    def peer(x, layer, tm):
        wqt = peer_w_q[layer].T.astype(BF16)
        u = peer_u[layer].astype(BF16)
        vt = peer_v[layer].T.astype(BF16)
        return _peer_ffn_ln(x, wqt, peer_keys1[layer].astype(BF16), peer_keys2[layer].astype(BF16),
                            u, vt, row(ln_ffn_g[layer]), row(ln_ffn_b[layer]),
                            alpha=alpha, tm=tm, te=PEER_TE)

    def peer_sample(x, layer):
        xpad = jnp.pad(x, ((0, SAMPLE_PAD - nb), (0, 0)))
        return peer(xpad, layer, SAMPLE_PAD)[:nb]

    ck = cache_k.reshape(n_attn, n_phys, span, HEAD_DIM)
    cv = cache_v.reshape(n_attn, n_phys, span, HEAD_DIM)
    page_rows = PAGE_CUMSUM_ROWS if (n_attn * n_phys) % PAGE_CUMSUM_ROWS == 0 else n_attn * n_phys
    page_cs, page_tot = _page_cumsum(cache_logf.reshape(n_attn * n_phys, span), nh, rows=page_rows)
    page_cs = page_cs.reshape(n_attn, n_phys, 1, span)
    page_tot = page_tot.reshape(n_attn, n_phys, 1, span)

    kp_l, vp_l, lp_l, ks_l, vs_l, ls_l, pp_l, ps_l = [], [], [], [], [], [], [], []
    for i in range(depth):
        j = i // 2
        g_mix, b_mix = row(ln_mix_g[i]), row(ln_mix_b[i])
        if i % 2 == 0:
            w_in = attn_w_in[j]
            wq = w_in[:, :d].astype(BF16)
            wk = w_in[:, d:2 * d].astype(BF16)
            wv = w_in[:, 2 * d:3 * d].astype(BF16)
            wf = w_in[:, 3 * d:].astype(BF16)
            b_f = attn_b_f[j]
            proj = functools.partial(_fox_proj, wq=wq, wk=wk, wv=wv, wf=wf, wft=wf.T,
                                     b_f=b_f.reshape(1, nh), b_ft=b_f.reshape(nh, 1))
            wo = attn_w_o[j].astype(BF16)

            qb, k, kb, v, vb, lf, lft, qn2, kn2 = proj(xp, tm=PROMPT_TM, tn=512)
            cst = _cumsum_rows_of(lft)
            need = _needed_key_blocks(qn2, kn2, cst, FLASH_BLOCK)
            o = _flash_attention(need, qb, kb, vb, cst, blk=FLASH_BLOCK)
            xp = _proj_ln(o, wo, xp, g_mix, b_mix, alpha=alpha, tm=256)
            kp_l.append(k.reshape(1, seq, nh, HEAD_DIM))
            vp_l.append(v.reshape(1, seq, nh, HEAD_DIM))
            lp_l.append(lf.reshape(1, seq, nh))

            qb, k, kb, v, vb, lf, lft, _, _ = proj(xs, tm=nb, tn=512)
            heads = lambda t: t.reshape(nb, nh, HEAD_DIM)
            o = _decode_attention(page_table, heads(qb), heads(kb), heads(vb), lf, ck, cv,
                                  page_cs, page_tot, j, ppb=DECODE_PAGES)
            xs = _proj_ln(o.reshape(nb, d), wo, xs, g_mix, b_mix, alpha=alpha, tm=nb)
            ks_l.append(k.reshape(nb, 1, nh, HEAD_DIM))
            vs_l.append(v.reshape(nb, 1, nh, HEAD_DIM))
            ls_l.append(lf.reshape(nb, 1, nh))
        else:
            pw = pool_w[j].astype(BF16)
            psc = row(pool_scale[j])
            prefix = jnp.zeros((POOL_STATE, d), xp.dtype)
            pp_l.append(jnp.concatenate([prefix, xp], axis=0)[-POOL_STATE:][None])
            xx = jnp.concatenate([state_pool[j], xs[:, None, :]], axis=1)
            ps_l.append(xx[:, -POOL_STATE:])
            xp = _pool_prompt(xp, pw, psc, g_mix, b_mix, alpha=alpha, tm=PROMPT_TM)
            xs = _pool_sample(jnp.swapaxes(xx, 0, 1), pw, psc, g_mix, b_mix, pos0=past_len, alpha=alpha)
        xp = peer(xp, i, PROMPT_TM)
        xs = peer_sample(xs, i)
    return (xp[None], xs[:, None, :], jnp.stack(kp_l), jnp.stack(vp_l), jnp.stack(lp_l),
            jnp.stack(ks_l), jnp.stack(vs_l), jnp.stack(ls_l), jnp.stack(pp_l), jnp.stack(ps_l))
```

```python
import functools

import jax
import jax.numpy as jnp
from jax import lax
from jax.experimental import pallas as pl
from jax.experimental.pallas import tpu as pltpu

BF16 = jnp.bfloat16
F32 = jnp.float32

HEAD_DIM = 128
PAGE_SIZE = 128
ATTN_SCALE = HEAD_DIM ** -0.5
NEG_INF = -1e30
POOL_WINDOWS = (2, 4, 8, 16)
POOL_STATE = max(POOL_WINDOWS) - 1
PEER_HEADS = 8
PEER_NKEYS = 128
PEER_TOPK = 16
PEER_QDIM = 256
PEER_QHALF = PEER_QDIM // 2
LN_EPS = 1e-5

LANES = 128
SUBLANES = 8
V7X_VMEM_BYTES = 64 * 1024 * 1024
VMEM_LIMIT = V7X_VMEM_BYTES * 7 // 8

NT_DIMS = (((1,), (1,)), ((), ()))

SKIP_LOGIT = -100.0


def _params(semantics):
    return pltpu.CompilerParams(dimension_semantics=semantics, vmem_limit_bytes=VMEM_LIMIT)


def _layer_norm(y, g, b):
    mu = jnp.mean(y, axis=-1, keepdims=True)
    d = y - mu
    var = jnp.mean(d * d, axis=-1, keepdims=True)
    return d * lax.rsqrt(var + LN_EPS) * g + b


def _log_sigmoid(z):
    return jnp.minimum(z, 0.0) - jnp.log1p(jnp.exp(-jnp.abs(z)))


def _cumsum_lanes(c, first_shift=1):
    n = c.shape[-1]
    lane = lax.broadcasted_iota(jnp.int32, c.shape, c.ndim - 1)
    s = first_shift
    while s < n:
        c = c + jnp.where(lane >= s, pltpu.roll(c, s, axis=c.ndim - 1), 0.0)
        s *= 2
    return c


def _fox_proj_kernel(x_ref, wq_ref, wk_ref, wv_ref, wf_ref, wft_ref, bf_ref, bft_ref,
                     qb_ref, k_ref, kb_ref, v_ref, vb_ref, lf_ref, lft_ref, qn2_ref, kn2_ref):
    xb = x_ref[...].astype(BF16)
    tn, heads = wq_ref.shape[1], qn2_ref.shape[2]
    seg = (lax.broadcasted_iota(jnp.int32, (tn, heads), 0) // HEAD_DIM
           == lax.broadcasted_iota(jnp.int32, (tn, heads), 1)).astype(BF16)
    qb = jnp.dot(xb, wq_ref[...], preferred_element_type=F32).astype(BF16)
    qb_ref[...] = qb
    qn2_ref[0] = jnp.dot(qb * qb, seg, preferred_element_type=F32)
    k = jnp.dot(xb, wk_ref[...], preferred_element_type=F32)
    k_ref[...] = k
    kb = k.astype(BF16)
    kb_ref[...] = kb
    kn2_ref[0] = jnp.dot(kb * kb, seg, preferred_element_type=F32)
    v = jnp.dot(xb, wv_ref[...], preferred_element_type=F32)
    v_ref[...] = v
    vb_ref[...] = v.astype(BF16)

    @pl.when(pl.program_id(1) == 0)
    def _():
        z = jnp.dot(xb, wf_ref[...], preferred_element_type=F32) + bf_ref[...]
        lf_ref[...] = _log_sigmoid(z)
        zt = lax.dot_general(wft_ref[...], xb, NT_DIMS, preferred_element_type=F32) + bft_ref[...]
        lft_ref[...] = _log_sigmoid(zt)


def _fox_proj(x, wq, wk, wv, wf, wft, b_f, b_ft, *, tm, tn):
    n, d = x.shape
    h = wf.shape[1]
    hpt = tn // HEAD_DIM
    row_col = lambda i, j: (i, j)
    norms = jax.ShapeDtypeStruct((d // tn, n, hpt), F32)
    norm_spec = pl.BlockSpec((1, tm, hpt), lambda i, j: (j, i, 0))
    return pl.pallas_call(
        _fox_proj_kernel,
        grid=(n // tm, d // tn),
        in_specs=[
            pl.BlockSpec((tm, d), lambda i, j: (i, 0)),
            pl.BlockSpec((d, tn), lambda i, j: (0, j)),
            pl.BlockSpec((d, tn), lambda i, j: (0, j)),
            pl.BlockSpec((d, tn), lambda i, j: (0, j)),
            pl.BlockSpec((d, h), lambda i, j: (0, 0)),
            pl.BlockSpec((h, d), lambda i, j: (0, 0)),
            pl.BlockSpec((1, h), lambda i, j: (0, 0)),
            pl.BlockSpec((h, 1), lambda i, j: (0, 0)),
        ],
        out_specs=[
            pl.BlockSpec((tm, tn), row_col),
            pl.BlockSpec((tm, tn), row_col),
            pl.BlockSpec((tm, tn), row_col),
            pl.BlockSpec((tm, tn), row_col),
            pl.BlockSpec((tm, tn), row_col),
            pl.BlockSpec((tm, h), lambda i, j: (i, 0)),
            pl.BlockSpec((h, tm), lambda i, j: (0, i)),
            norm_spec,
            norm_spec,
        ],
        out_shape=[
            jax.ShapeDtypeStruct((n, d), BF16),
            jax.ShapeDtypeStruct((n, d), F32),
            jax.ShapeDtypeStruct((n, d), BF16),
            jax.ShapeDtypeStruct((n, d), F32),
            jax.ShapeDtypeStruct((n, d), BF16),
            jax.ShapeDtypeStruct((n, h), F32),
            jax.ShapeDtypeStruct((h, n), F32),
            norms,
            norms,
        ],
        compiler_params=_params(("parallel", "arbitrary")),
        name="fox_proj",
    )(x, wq, wk, wv, wf, wft, b_f, b_ft)


def _cumsum_kernel(x_ref, o_ref):
    o_ref[...] = _cumsum_lanes(x_ref[...])


def _cumsum_rows_of(xt):
    return pl.pallas_call(
        _cumsum_kernel,
        out_shape=jax.ShapeDtypeStruct(xt.shape, F32),
        compiler_params=pltpu.CompilerParams(vmem_limit_bytes=VMEM_LIMIT),
        name="logf_cumsum",
    )(xt)


def _flash_kernel(need_ref, q_ref, k_ref, v_ref, c_ref, o_ref, m_sc, l_sc, acc_sc, *, blk):
    hg = pl.program_id(0)
    qi = pl.program_id(1)
    hps = m_sc.shape[0]
    m_sc[...] = jnp.full(m_sc.shape, -jnp.inf, F32)
    l_sc[...] = jnp.zeros(l_sc.shape, F32)
    acc_sc[...] = jnp.zeros(acc_sc.shape, F32)
    q0 = pl.multiple_of(qi * blk, blk)

    def block(kj, diagonal):
        k0 = pl.multiple_of(kj * blk, blk)
        for hh in range(hps):
            feat = slice(hh * HEAD_DIM, (hh + 1) * HEAD_DIM)
            s = lax.dot_general(q_ref[:, feat], k_ref[pl.ds(k0, blk), feat], NT_DIMS,
                                preferred_element_type=F32) * ATTN_SCALE
            c_q0 = c_ref[hh, :, pl.ds(q0, LANES)][:, 0:1]
            s = s + (c_q0 - c_ref[hh, :, pl.ds(k0, blk)])
            if diagonal:
                rows = lax.broadcasted_iota(jnp.int32, (blk, blk), 0)
                cols = lax.broadcasted_iota(jnp.int32, (blk, blk), 1)
                s = jnp.where(cols <= rows, s, NEG_INF)
            m_old = m_sc[hh]
            m_new = jnp.maximum(m_old, jnp.max(s, axis=1, keepdims=True))
            a = jnp.exp(m_old - m_new)
            p = jnp.exp(s - jnp.tile(m_new, (1, blk // LANES))).astype(BF16)
            v_ones = jnp.concatenate([v_ref[pl.ds(k0, blk), feat], jnp.ones((blk, LANES), BF16)], axis=1)
            pv = jnp.dot(p, v_ones, preferred_element_type=F32)
            acc_sc[hh] = a * acc_sc[hh] + pv[:, :HEAD_DIM]
            l_sc[hh] = a * l_sc[hh] + pv[:, HEAD_DIM:]
            m_sc[hh] = m_new

    needed = need_ref[hg * hps * pl.num_programs(1) + qi]
    for hh in range(1, hps):
        needed = needed | need_ref[(hg * hps + hh) * pl.num_programs(1) + qi]

    def body(kj, carry):
        @pl.when(((needed >> kj) & 1) != 0)
        def _():
            block(kj, False)
        return carry

    lax.fori_loop(0, qi, body, 0)
    block(qi, True)
    for hh in range(hps):
        o_ref[:, hh * HEAD_DIM:(hh + 1) * HEAD_DIM] = (acc_sc[hh] / l_sc[hh]).astype(o_ref.dtype)


def _flash_attention(need, qb, kb, vb, cst, *, blk, hps):
    n, d = qb.shape
    nh = d // HEAD_DIM
    width = hps * HEAD_DIM
    grid_spec = pltpu.PrefetchScalarGridSpec(
        num_scalar_prefetch=1,
        grid=(nh // hps, n // blk),
        in_specs=[
            pl.BlockSpec((blk, width), lambda hg, qi, need: (qi, hg)),
            pl.BlockSpec((n, width), lambda hg, qi, need: (0, hg)),
            pl.BlockSpec((n, width), lambda hg, qi, need: (0, hg)),
            pl.BlockSpec((hps, 1, n), lambda hg, qi, need: (hg, 0, 0)),
        ],
        out_specs=pl.BlockSpec((blk, width), lambda hg, qi, need: (qi, hg)),
        scratch_shapes=[
            pltpu.VMEM((hps, blk, LANES), F32),
            pltpu.VMEM((hps, blk, LANES), F32),
            pltpu.VMEM((hps, blk, HEAD_DIM), F32),
        ],
    )
    return pl.pallas_call(
        functools.partial(_flash_kernel, blk=blk),
        grid_spec=grid_spec,
        out_shape=jax.ShapeDtypeStruct((n, d), BF16),
        compiler_params=_params(("parallel", "arbitrary")),
        name="fox_flash",
    )(need, qb, kb, vb, cst.reshape(nh, 1, n))


def _needed_key_blocks(qn2, kn2, cst, blk):
    nh, n = cst.shape
    nblk = n // blk
    assert nblk <= 32

    def block_norm(x2):
        x2 = x2.transpose(1, 0, 2).reshape(nblk, blk, nh)
        return jnp.sqrt(jnp.max(x2, axis=1)).T

    qn, kn = block_norm(qn2), block_norm(kn2)
    c_first = cst[:, ::blk]
    c_last = cst[:, blk - 1::blk]
    bound = (ATTN_SCALE * qn[:, :, None] * (kn[:, None, :] + kn[:, :, None])
             + c_first[:, :, None] - c_last[:, None, :])
    bits = (bound >= SKIP_LOGIT).astype(jnp.uint32) << jnp.arange(nblk, dtype=jnp.uint32)
    return lax.bitcast_convert_type(jnp.sum(bits, axis=2, dtype=jnp.uint32), jnp.int32).reshape(-1)


def _page_cumsum_kernel(x_ref, cs_ref, tot_ref, *, nh):
    x = x_ref[...]
    n = x.shape[1]
    cs = _cumsum_lanes(x, first_shift=nh)
    cs_ref[...] = cs
    lane = lax.broadcasted_iota(jnp.int32, x.shape, 1)
    t = jnp.where(lane >= n - nh, cs, 0.0)
    s = nh
    while s < n:
        t = t + pltpu.roll(t, s, axis=1)
        s *= 2
    tot_ref[...] = t


def _page_cumsum(lf, nh, *, rows):
    npg, span = lf.shape
    spec = pl.BlockSpec((rows, span), lambda i: (i, 0))
    shape = jax.ShapeDtypeStruct((npg, span), F32)
    return pl.pallas_call(
        functools.partial(_page_cumsum_kernel, nh=nh),
        grid=(npg // rows,),
        in_specs=[spec],
        out_specs=[spec, spec],
        out_shape=[shape, shape],
        compiler_params=_params(("parallel",)),
        name="page_cumsum",
    )(lf)


def _decode_kernel(pt_ref, q_ref, kn_ref, vn_ref, lfn_ref, *rest, ppb):
    del pt_ref
    k_refs, v_refs = rest[0:ppb], rest[ppb:2 * ppb]
    cs_refs, tot_refs = rest[2 * ppb:3 * ppb], rest[3 * ppb:4 * ppb]
    o_ref, m_sc, l_sc, acc_sc, c_sc = rest[4 * ppb:]
    g = pl.program_id(1)
    nh = acc_sc.shape[0]
    span = c_sc.shape[1]
    row = lax.broadcasted_iota(jnp.int32, (nh, span), 0)
    lane = lax.broadcasted_iota(jnp.int32, (nh, span), 1)
    own = (lane % nh) == row
    qm = q_ref[0]

    @pl.when(g == 0)
    def _():
        m_sc[...] = jnp.full(m_sc.shape, -jnp.inf, F32)
        l_sc[...] = jnp.zeros(l_sc.shape, F32)
        acc_sc[...] = jnp.zeros(acc_sc.shape, F32)
        c_sc[...] = jnp.zeros(c_sc.shape, F32)

    def update(s, pv_of):
        m_old = m_sc[...]
        m_new = jnp.maximum(m_old, jnp.max(s, axis=1, keepdims=True))
        a = jnp.exp(m_old - m_new)
        pe = jnp.exp(s - m_new)
        l_sc[...] = a * l_sc[...] + jnp.sum(pe, axis=1, keepdims=True)
        acc_sc[...] = a * acc_sc[...] + pv_of(pe)
        m_sc[...] = m_new

    for i in range(ppb):
        r = lax.dot_general(qm, k_refs[i][...].astype(BF16), NT_DIMS,
                            preferred_element_type=F32) * ATTN_SCALE
        bias = cs_refs[i][...] + c_sc[...]
        c_sc[...] += tot_refs[i][...]
        v_page = v_refs[i]
        update(jnp.where(own, r - bias, -jnp.inf),
               lambda pe: jnp.dot(pe.astype(BF16), v_page[...].astype(BF16), preferred_element_type=F32))

    @pl.when(g == pl.num_programs(1) - 1)
    def _():
        s_new = jnp.sum(qm.astype(F32) * kn_ref[0].astype(F32), axis=1, keepdims=True) * ATTN_SCALE
        c_col = jnp.sum(jnp.where(jnp.logical_and(own, lane < nh), c_sc[...], 0.0), axis=1, keepdims=True)
        s_new = s_new - (c_col + lfn_ref[0])
        update(s_new, lambda pe: pe.astype(BF16).astype(F32) * vn_ref[0].astype(F32))
        o_ref[0] = (acc_sc[...] / l_sc[...]).astype(o_ref.dtype)


def _decode_attention(page_table, qb, kb_new, vb_new, logf_new, cache_k, cache_v, cs, tot, layer, *, ppb):
    nb, nh, hd = qb.shape
    span = cache_k.shape[2]
    n_pages = page_table.shape[1]
    per_seq = lambda b, g, pt: (b, 0, 0)

    def page(i):
        return lambda b, g, pt: (layer, pt[b, g * ppb + i], 0, 0)

    kv_specs = [pl.BlockSpec((None, None, span, hd), page(i)) for i in range(ppb)]
    lf_specs = [pl.BlockSpec((None, None, 1, span), page(i)) for i in range(ppb)]
    grid_spec = pltpu.PrefetchScalarGridSpec(
        num_scalar_prefetch=1,
        grid=(nb, n_pages // ppb),
        in_specs=[
            pl.BlockSpec((1, nh, hd), per_seq),
            pl.BlockSpec((1, nh, hd), per_seq),
            pl.BlockSpec((1, nh, hd), per_seq),
            pl.BlockSpec((1, nh, 1), per_seq),
        ] + kv_specs + kv_specs + lf_specs + lf_specs,
        out_specs=pl.BlockSpec((1, nh, hd), per_seq),
        scratch_shapes=[
            pltpu.VMEM((nh, 1), F32),
            pltpu.VMEM((nh, 1), F32),
            pltpu.VMEM((nh, hd), F32),
            pltpu.VMEM((1, span), F32),
        ],
    )
    return pl.pallas_call(
        functools.partial(_decode_kernel, ppb=ppb),
        grid_spec=grid_spec,
        out_shape=jax.ShapeDtypeStruct((nb, nh, hd), BF16),
        compiler_params=_params(("parallel", "arbitrary")),
        name="fox_decode",
    )(page_table, qb, kb_new, vb_new, logf_new[:, :, None],
      *([cache_k] * ppb), *([cache_v] * ppb), *([cs] * ppb), *([tot] * ppb))


def _proj_ln_kernel(a_ref, w_ref, x_ref, g_ref, b_ref, o_ref, *, alpha):
    y = jnp.dot(a_ref[...], w_ref[...], preferred_element_type=F32)
    o_ref[...] = _layer_norm(alpha * x_ref[...] + y, g_ref[...], b_ref[...])


def _proj_ln(a, w, x, g, b, *, alpha, tm):
    n, d = x.shape
    rows = lambda i: (i, 0)
    fixed = lambda i: (0, 0)
    return pl.pallas_call(
        functools.partial(_proj_ln_kernel, alpha=alpha),
        grid=(n // tm,),
        in_specs=[
            pl.BlockSpec((tm, a.shape[1]), rows),
            pl.BlockSpec(w.shape, fixed),
            pl.BlockSpec((tm, d), rows),
            pl.BlockSpec((1, d), fixed),
            pl.BlockSpec((1, d), fixed),
        ],
        out_specs=pl.BlockSpec((tm, d), rows),
        out_shape=jax.ShapeDtypeStruct((n, d), F32),
        compiler_params=_params(("parallel",)),
        name="proj_ln",
    )(a, w, x, g, b)


HALO = POOL_STATE + 1


def _pool_prompt_kernel(x_ref, w_ref, sc_ref, g_ref, b_ref, o_ref, buf, *, tm, alpha):
    i = pl.program_id(0)
    gd = w_ref.shape[1]

    @pl.when(i == 0)
    def _():
        buf[0:HALO, :] = jnp.zeros((HALO, buf.shape[1]), F32)

    @pl.when(i > 0)
    def _():
        buf[0:HALO, :] = buf[tm:tm + HALO, :]

    x = x_ref[...]
    buf[HALO:HALO + tm, :] = x
    pos = i * tm + lax.broadcasted_iota(jnp.int32, (tm, 1), 0)
    ys = []
    for g, w in enumerate(POOL_WINDOWS):
        cols = slice(g * gd, (g + 1) * gd)
        win = x[:, cols]
        for back in range(1, w):
            win = win + buf[HALO - back:HALO - back + tm, cols]
        cnt = jnp.minimum(pos + 1, w).astype(F32)
        z = win / cnt - x[:, cols]
        ys.append(jnp.dot(z.astype(BF16), w_ref[g], preferred_element_type=F32))
    y = jnp.concatenate(ys, axis=1) * sc_ref[...]
    o_ref[...] = _layer_norm(alpha * x + y, g_ref[...], b_ref[...])


def _pool_prompt(x, w, scale, g, b, *, alpha, tm):
    n, d = x.shape
    rows = lambda i: (i, 0)
    fixed = lambda i: (0, 0)
    return pl.pallas_call(
        functools.partial(_pool_prompt_kernel, tm=tm, alpha=alpha),
        grid=(n // tm,),
        in_specs=[
            pl.BlockSpec((tm, d), rows),
            pl.BlockSpec(w.shape, lambda i: (0, 0, 0)),
            pl.BlockSpec((1, d), fixed),
            pl.BlockSpec((1, d), fixed),
            pl.BlockSpec((1, d), fixed),
        ],
        out_specs=pl.BlockSpec((tm, d), rows),
        out_shape=jax.ShapeDtypeStruct((n, d), F32),
        scratch_shapes=[pltpu.VMEM((HALO + tm, d), F32)],
        compiler_params=_params(("arbitrary",)),
        name="pool_prompt",
    )(x, w, scale, g, b)


def _pool_sample_kernel(xx_ref, w_ref, sc_ref, g_ref, b_ref, o_ref, *, pos0, alpha):
    gd = w_ref.shape[1]
    rows = xx_ref.shape[0]
    x = xx_ref[rows - 1]
    ys = []
    for g, w in enumerate(POOL_WINDOWS):
        cols = slice(g * gd, (g + 1) * gd)
        win = x[:, cols]
        for back in range(1, w):
            win = win + xx_ref[rows - 1 - back, :, cols]
        z = win / float(min(pos0 + 1, w)) - x[:, cols]
        ys.append(jnp.dot(z.astype(BF16), w_ref[g], preferred_element_type=F32))
    y = jnp.concatenate(ys, axis=1) * sc_ref[...]
    o_ref[...] = _layer_norm(alpha * x + y, g_ref[...], b_ref[...])


def _pool_sample(xx, w, scale, g, b, *, pos0, alpha):
    _, nb, d = xx.shape
    return pl.pallas_call(
        functools.partial(_pool_sample_kernel, pos0=pos0, alpha=alpha),
        out_shape=jax.ShapeDtypeStruct((nb, d), F32),
        compiler_params=pltpu.CompilerParams(vmem_limit_bytes=VMEM_LIMIT),
        name="pool_sample",
    )(xx, w, scale, g, b)


def _oddeven_merge(lo, hi, r):
    step = r * 2
    if step < hi - lo:
        yield from _oddeven_merge(lo, hi, step)
        yield from _oddeven_merge(lo + r, hi, step)
        yield from [(i, i + r) for i in range(lo + r, hi - r, step)]
    else:
        yield (lo, lo + r)


def _oddeven_merge_sort(lo, hi):
    if hi - lo >= 1:
        mid = lo + (hi - lo) // 2
        yield from _oddeven_merge_sort(lo, mid)
        yield from _oddeven_merge_sort(mid + 1, hi)
        yield from _oddeven_merge(lo, hi, 1)


SORT_NETWORK = tuple(_oddeven_merge_sort(0, PEER_TOPK - 1))


def _sort_desc(xs):
    xs = list(xs)
    for i, j in SORT_NETWORK:
        a, b = xs[i], xs[j]
        if b is None:
            continue
        if a is None:
            xs[i], xs[j] = b, None
        else:
            xs[i], xs[j] = jnp.maximum(a, b), jnp.minimum(a, b)
    return xs


def _merge_sublanes(xs):
    k = len(xs)
    for shift in (4, 2, 1):
        ys = []
        for i in range(k):
            mine, theirs = xs[i], xs[k - 1 - i]
            if theirs is None:
                ys.append(mine)
            elif mine is None:
                ys.append(pltpu.roll(theirs, shift, axis=0))
            else:
                ys.append(jnp.maximum(mine, pltpu.roll(theirs, shift, axis=0)))
        stride = k // 2
        while stride >= 1:
            for i in range(k):
                if i & stride == 0:
                    a, b = ys[i], ys[i + stride]
                    ys[i], ys[i + stride] = jnp.maximum(a, b), jnp.minimum(a, b)
            stride //= 2
        xs = ys
    return xs


def _top16_rows(s):
    return _merge_sublanes(_sort_desc([s[SUBLANES * j:SUBLANES * (j + 1), :] for j in range(PEER_TOPK)]))


def _peer_stats_group(s1, s2):
    ln = s1.shape[1]
    t1 = _top16_rows(s1)
    t2 = _top16_rows(s2)
    sub = lax.broadcasted_iota(jnp.int32, (SUBLANES, ln), 0)

    def spread(ts):
        out = ts[0]
        for r in range(1, SUBLANES):
            out = jnp.where(sub == r, ts[r], out)
        return out

    t2_lo, t2_hi, t1_hi = spread(t2[:SUBLANES]), spread(t2[SUBLANES:]), spread(t1[SUBLANES:])
    cands = [t1[0] + t2_lo, t1[0] + t2_hi, t1[1] + t2_lo]
    for i in range(2, SUBLANES):
        cands.append(jnp.where(sub < PEER_TOPK // (i + 1), t1[i] + t2_lo, -jnp.inf))
    cands.append(t1_hi + t2[0])
    top = _merge_sublanes(_sort_desc(cands + [None] * (PEER_TOPK - len(cands))))
    mx, tau = top[0], top[PEER_TOPK - 1]
    z = jnp.where(cands[0] >= tau, jnp.exp(cands[0] - mx), 0.0)
    for c in cands[1:]:
        z = z + jnp.where(c >= tau, jnp.exp(c - mx), 0.0)
    z = jnp.sum(z, axis=0, keepdims=True)
    w1 = jnp.exp(s1 - t1[0][0:1]) / z
    e2 = jnp.exp(s2 - t2[0][0:1])
    rank2 = jnp.zeros(s2.shape, F32)
    for j in range(PEER_TOPK):
        rank2 = rank2 + jnp.where(s2 < t2[j][0:1], 1.0, 0.0)
    count1 = jnp.zeros(s1.shape, F32)
    for i in reversed(range(PEER_TOPK)):
        reach = jnp.where(t1[i] + t2_lo >= tau, 1.0, 0.0) + jnp.where(t1[i] + t2_hi >= tau, 1.0, 0.0)
        count1 = jnp.where(s1 == t1[i][0:1], jnp.sum(reach, axis=0, keepdims=True), count1)
    return w1, e2, rank2, count1


def _peer_stats_kernel(x_ref, wqt_ref, k1_ref, k2_ref, cnt_ref, w1_ref, rank_ref, e2_ref, qt_sc, s1_sc, s2_sc):
    h = pl.program_id(1)

    @pl.when(h == 0)
    def _():
        qt = lax.dot_general(wqt_ref[...], x_ref[...].astype(BF16), NT_DIMS, preferred_element_type=F32)
        qt_sc[...] = qt.astype(BF16)

    base = pl.multiple_of(h * PEER_QDIM, PEER_QDIM)
    q1 = qt_sc[pl.ds(base, PEER_QHALF), :]
    q2 = qt_sc[pl.ds(base + PEER_QHALF, PEER_QHALF), :]
    s1_sc[...] = jnp.dot(k1_ref[0], q1, preferred_element_type=F32)
    s2_sc[...] = jnp.dot(k2_ref[0], q2, preferred_element_type=F32)
    for grp in range(s1_sc.shape[1] // LANES):
        lanes = slice(grp * LANES, (grp + 1) * LANES)
        w1, e2, rank2, count1 = _peer_stats_group(s1_sc[:, lanes], s2_sc[:, lanes])
        w1_ref[0, :, lanes] = w1
        cnt_ref[0, :, lanes] = count1
        e2_ref[0, :, lanes] = e2.astype(BF16)
        rank_ref[0, :, lanes] = rank2.astype(BF16)


def _peer_stats(x, wqt, k1, k2, *, tm):
    n, d = x.shape
    nh, nk, _ = k1.shape
    assert nk == PEER_TOPK * SUBLANES
    per_head = lambda i, h: (h, 0, i)
    keys = lambda i, h: (h, 0, 0)
    full = jax.ShapeDtypeStruct((nh, nk, n), F32)
    half = jax.ShapeDtypeStruct((nh, nk, n), BF16)
    return pl.pallas_call(
        _peer_stats_kernel,
        grid=(n // tm, nh),
        in_specs=[
            pl.BlockSpec((tm, d), lambda i, h: (i, 0)),
            pl.BlockSpec(wqt.shape, lambda i, h: (0, 0)),
            pl.BlockSpec((1, nk, k1.shape[2]), keys),
            pl.BlockSpec((1, nk, k2.shape[2]), keys),
        ],
        out_specs=[
            pl.BlockSpec((1, nk, tm), per_head),
            pl.BlockSpec((1, nk, tm), per_head),
            pl.BlockSpec((1, nk, tm), per_head),
            pl.BlockSpec((1, nk, tm), per_head),
        ],
        out_shape=[full, full, half, half],
        scratch_shapes=[pltpu.VMEM((wqt.shape[0], tm), BF16), pltpu.VMEM((nk, tm), F32),
                        pltpu.VMEM((nk, tm), F32)],
        compiler_params=_params(("parallel", "arbitrary")),
        name="peer_stats",
    )(x, wqt, k1, k2)


def _gelu(h):
    return 0.5 * h * (1.0 + lax.erf(h * (2.0 ** -0.5)))


BF16_ROWS = 16


def _peer_main_kernel(x_ref, u_ref, vt_ref, cnt_ref, w1_ref, rank_ref, e2_ref, g_ref, b_ref,
                      o_ref, xb_sc, hid_sc, act_sc, acc_sc, *, alpha):
    j = pl.program_id(1)
    nh, nk, tm = rank_ref.shape
    chunk = u_ref.shape[0] // nk

    def bf16_rows(row):
        return jnp.tile(jnp.broadcast_to(row, (BF16_ROWS, LANES)).astype(BF16), (nk // BF16_ROWS, 1))

    @pl.when(j == 0)
    def _():
        xb_sc[...] = x_ref[...].astype(BF16)
        acc_sc[...] = jnp.zeros(acc_sc.shape, F32)

    hid_sc[...] = lax.dot_general(u_ref[...], xb_sc[...], NT_DIMS, preferred_element_type=F32)
    zero = jnp.zeros((), BF16)
    one = jnp.ones((), BF16)
    for aa in range(chunk):
        rows = slice(aa * nk, (aa + 1) * nk)
        for grp in range(tm // LANES):
            lanes = slice(grp * LANES, (grp + 1) * LANES)
            gate = None
            for h in range(nh):
                count = bf16_rows(cnt_ref[h, aa:aa + 1, lanes])
                w1 = bf16_rows(w1_ref[h, aa:aa + 1, lanes])
                keep = jnp.minimum(jnp.maximum(count - rank_ref[h, :, lanes], zero), one)
                term = (w1 * e2_ref[h, :, lanes]) * keep
                gate = term if gate is None else gate + term
            act_sc[rows, lanes] = _gelu(hid_sc[rows, lanes]).astype(BF16) * gate
    acc_sc[...] += jnp.dot(vt_ref[...], act_sc[...], preferred_element_type=F32)

    @pl.when(j == pl.num_programs(1) - 1)
    def _():
        y = acc_sc[...].T
        o_ref[...] = _layer_norm(alpha * x_ref[...] + y, g_ref[...], b_ref[...])


def _peer_main(x, u, vt, cnt, w1, rank, e2, g, b, *, layer, alpha, tm, te):
    n, d = x.shape
    ne = u.shape[1]
    nh, nk, _ = rank.shape
    chunk = te // nk
    tok = lambda i, j: (0, 0, i)
    first_key = lambda i, j: (0, j, i)
    fixed = lambda i, j: (0, 0)
    return pl.pallas_call(
        functools.partial(_peer_main_kernel, alpha=alpha),
        grid=(n // tm, ne // te),
        in_specs=[
            pl.BlockSpec((tm, d), lambda i, j: (i, 0), pipeline_mode=pl.Buffered(1)),
            pl.BlockSpec((None, te, d), lambda i, j: (layer, j, 0)),
            pl.BlockSpec((None, d, te), lambda i, j: (layer, 0, j)),
            pl.BlockSpec((nh, chunk, tm), first_key),
            pl.BlockSpec((nh, chunk, tm), first_key),
            pl.BlockSpec((nh, nk, tm), tok),
            pl.BlockSpec((nh, nk, tm), tok),
            pl.BlockSpec((1, d), fixed),
            pl.BlockSpec((1, d), fixed),
        ],
        out_specs=pl.BlockSpec((tm, d), lambda i, j: (i, 0)),
        out_shape=jax.ShapeDtypeStruct((n, d), F32),
        scratch_shapes=[
            pltpu.VMEM((tm, d), BF16),
            pltpu.VMEM((te, tm), F32),
            pltpu.VMEM((te, tm), BF16),
            pltpu.VMEM((d, tm), F32),
        ],
        compiler_params=_params(("parallel", "arbitrary")),
        name="peer_main",
    )(x, u, vt, cnt, w1, rank, e2, g, b)


def _peer_ffn_ln(x, wqt, k1, k2, u, vt, g, b, *, layer, alpha, tm, te):
    cnt, w1, rank, e2 = _peer_stats(x, wqt, k1, k2, tm=tm)
    return _peer_main(x, u, vt, cnt, w1, rank, e2, g, b, layer=layer, alpha=alpha, tm=tm, te=te)


PROMPT_TM = 512
FLASH_BLOCK = 256
PEER_TE = 1024
SAMPLE_PAD = LANES
FLASH_HEADS = 2
DECODE_PAGES = 8
PAGE_CUMSUM_ROWS = 256


def kernel(x_prompt, x_sample, cache_k, cache_v, cache_logf, page_table, state_pool, attn_w_in, attn_b_f,
           attn_w_o, pool_w, pool_scale, peer_w_q, peer_keys1, peer_keys2, peer_u, peer_v, ln_mix_g,
           ln_mix_b, ln_ffn_g, ln_ffn_b):
    bsz, seq, d = x_prompt.shape
    nb, dec_seq, _ = x_sample.shape
    assert bsz == 1 and dec_seq == 1
    depth = ln_mix_g.shape[0]
    alpha = (2.0 * depth) ** 0.25
    nh = d // HEAD_DIM
    past_len = page_table.shape[1] * PAGE_SIZE
    n_attn, n_phys = cache_k.shape[0], cache_k.shape[1]
    span = PAGE_SIZE * nh

    xp = x_prompt[0]
    xs = x_sample[:, 0, :]
    row = lambda v: v.reshape(1, -1)

    u_all = peer_u.astype(BF16)
    vt_all = jnp.swapaxes(peer_v, 1, 2).astype(BF16)

    def peer(x, layer, tm):
        wqt = peer_w_q[layer].T.astype(BF16)
        return _peer_ffn_ln(x, wqt, peer_keys1[layer].astype(BF16), peer_keys2[layer].astype(BF16),
                            u_all, vt_all, row(ln_ffn_g[layer]), row(ln_ffn_b[layer]),
                            layer=layer, alpha=alpha, tm=tm, te=PEER_TE)

    def peer_sample(x, layer):
        xpad = jnp.pad(x, ((0, SAMPLE_PAD - nb), (0, 0)))
        return peer(xpad, layer, SAMPLE_PAD)[:nb]

    ck = cache_k.reshape(n_attn, n_phys, span, HEAD_DIM)
    cv = cache_v.reshape(n_attn, n_phys, span, HEAD_DIM)
    page_rows = PAGE_CUMSUM_ROWS if (n_attn * n_phys) % PAGE_CUMSUM_ROWS == 0 else n_attn * n_phys
    page_cs, page_tot = _page_cumsum(cache_logf.reshape(n_attn * n_phys, span), nh, rows=page_rows)
    page_cs = page_cs.reshape(n_attn, n_phys, 1, span)
    page_tot = page_tot.reshape(n_attn, n_phys, 1, span)

    kp_l, vp_l, lp_l, ks_l, vs_l, ls_l, pp_l, ps_l = [], [], [], [], [], [], [], []
    for i in range(depth):
        j = i // 2
        g_mix, b_mix = row(ln_mix_g[i]), row(ln_mix_b[i])
        if i % 2 == 0:
            w_in = attn_w_in[j]
            wq = w_in[:, :d].astype(BF16)
            wk = w_in[:, d:2 * d].astype(BF16)
            wv = w_in[:, 2 * d:3 * d].astype(BF16)
            wf = w_in[:, 3 * d:].astype(BF16)
            b_f = attn_b_f[j]
            proj = functools.partial(_fox_proj, wq=wq, wk=wk, wv=wv, wf=wf, wft=wf.T,
                                     b_f=b_f.reshape(1, nh), b_ft=b_f.reshape(nh, 1))
            wo = attn_w_o[j].astype(BF16)

            qb, k, kb, v, vb, lf, lft, qn2, kn2 = proj(xp, tm=PROMPT_TM, tn=512)
            cst = _cumsum_rows_of(lft)
            need = _needed_key_blocks(qn2, kn2, cst, FLASH_BLOCK)
            o = _flash_attention(need, qb, kb, vb, cst, blk=FLASH_BLOCK, hps=FLASH_HEADS)
            xp = _proj_ln(o, wo, xp, g_mix, b_mix, alpha=alpha, tm=256)
            kp_l.append(k.reshape(1, seq, nh, HEAD_DIM))
            vp_l.append(v.reshape(1, seq, nh, HEAD_DIM))
            lp_l.append(lf.reshape(1, seq, nh))

            qb, k, kb, v, vb, lf, lft, _, _ = proj(xs, tm=nb, tn=512)
            heads = lambda t: t.reshape(nb, nh, HEAD_DIM)
            o = _decode_attention(page_table, heads(qb), heads(kb), heads(vb), lf, ck, cv,
                                  page_cs, page_tot, j, ppb=DECODE_PAGES)
            xs = _proj_ln(o.reshape(nb, d), wo, xs, g_mix, b_mix, alpha=alpha, tm=nb)
            ks_l.append(k.reshape(nb, 1, nh, HEAD_DIM))
            vs_l.append(v.reshape(nb, 1, nh, HEAD_DIM))
            ls_l.append(lf.reshape(nb, 1, nh))
        else:
            pw = pool_w[j].astype(BF16)
            psc = row(pool_scale[j])
            prefix = jnp.zeros((POOL_STATE, d), xp.dtype)
            pp_l.append(jnp.concatenate([prefix, xp], axis=0)[-POOL_STATE:][None])
            xx = jnp.concatenate([state_pool[j], xs[:, None, :]], axis=1)
            ps_l.append(xx[:, -POOL_STATE:])
            xp = _pool_prompt(xp, pw, psc, g_mix, b_mix, alpha=alpha, tm=PROMPT_TM)
            xs = _pool_sample(jnp.swapaxes(xx, 0, 1), pw, psc, g_mix, b_mix, pos0=past_len, alpha=alpha)
        xp = peer(xp, i, PROMPT_TM)
        xs = peer_sample(xs, i)
    return (xp[None], xs[:, None, :], jnp.stack(kp_l), jnp.stack(vp_l), jnp.stack(lp_l),
            jnp.stack(ks_l), jnp.stack(vs_l), jnp.stack(ls_l), jnp.stack(pp_l), jnp.stack(ps_l))
```
